```python
import jax, jax.numpy as jnp
from jax import lax
import numpy as np

D_MODEL = 1024
BATCH = 2
SEQ = 16384
DEPTH = 2

N_MIXERS = 2
N_A_LAYERS = (DEPTH + 1) // 2
N_B_LAYERS = DEPTH // 2
RMS_EPS = 1e-6

GLA_HEADS = 4
GLA_DK = D_MODEL // 2 // GLA_HEADS
GLA_DV = D_MODEL // GLA_HEADS
GLA_GATE_RANK = 16
GLA_TAU = 16.0
GLA_CHUNK = 64
GLA_IN = 2 * GLA_HEADS * GLA_DK + 2 * GLA_HEADS * GLA_DV + GLA_GATE_RANK

SWA_HEAD_DIM = 64
SWA_Q_HEADS = D_MODEL // SWA_HEAD_DIM
SWA_KV_HEADS = 2
SWA_GROUP = SWA_Q_HEADS // SWA_KV_HEADS
SWA_WINDOW = 128
SWA_BLOCK = 128
ROPE_THETA = 500000.0
ROPE_DIM = SWA_HEAD_DIM // 4
SWA_IN = (SWA_Q_HEADS + 2 * SWA_KV_HEADS) * SWA_HEAD_DIM

PEER_HEADS = 8
PEER_N_KEYS = 128
PEER_N_EXPERTS = PEER_N_KEYS * PEER_N_KEYS
PEER_QDIM = 128
PEER_HALF = PEER_QDIM // 2
PEER_TOPK = 16
PEER_TOKEN_BLOCK = 128

kernel_name = "hybrid_gla_swa_peer_trunk"


def rms_norm(t, gain):
    t32 = t.astype(jnp.float32)
    y = t32 * lax.rsqrt(jnp.mean(t32 * t32, axis=-1, keepdims=True) + RMS_EPS)
    return (y * gain.astype(jnp.float32)).astype(t.dtype)


def gla_mixer(h, w_in, w_alpha, b_alpha, norm_gain, w_out):
    B, S, _ = h.shape
    H, dk, dv, C = GLA_HEADS, GLA_DK, GLA_DV, GLA_CHUNK
    proj = h @ w_in
    q, k, v, r, lr = jnp.split(proj, [H * dk, 2 * H * dk, 2 * H * dk + H * dv, 2 * H * dk + 2 * H * dv], axis=-1)
    log_a = jax.nn.log_sigmoid((lr @ w_alpha + b_alpha).astype(jnp.float32)) / GLA_TAU
    nc = S // C

    def to_chunks(t, d):
        return t.astype(jnp.float32).reshape(B, nc, C, H, d).transpose(1, 0, 3, 2, 4)

    qc = to_chunks(q, dk) * (dk ** -0.5)
    kc = to_chunks(k, dk)
    vc = to_chunks(v, dv)
    gc = to_chunks(log_a, dk)
    causal = jnp.tril(jnp.ones((C, C), dtype=bool))

    def step(state, xs):
        qb, kb, vb, gb = xs
        b = jnp.cumsum(gb, axis=-2)
        b_last = b[..., -1:, :]
        q_dec = qb * jnp.exp(b)
        k_inv = kb * jnp.exp(-b)
        attn = jnp.where(causal, jnp.einsum('bhid,bhjd->bhij', q_dec, k_inv), 0.0)
        o = jnp.einsum('bhij,bhjv->bhiv', attn, vb) + jnp.einsum('bhid,bhdv->bhiv', q_dec, state)
        k_fwd = kb * jnp.exp(b_last - b)
        state = jnp.exp(b_last[..., 0, :])[..., None] * state + jnp.einsum('bhjd,bhjv->bhdv', k_fwd, vb)
        return state, o

    s0 = jnp.zeros((B, H, dk, dv), jnp.float32)
    _, o = lax.scan(step, s0, (qc, kc, vc, gc))
    o = o.transpose(1, 0, 3, 2, 4).reshape(B, S, H, dv)
    o = rms_norm(o, norm_gain).reshape(B, S, H * dv)
    o = o * jax.nn.silu(r.astype(jnp.float32))
    return o.astype(h.dtype) @ w_out


def apply_partial_rotary(t, cos, sin):
    half = ROPE_DIM // 2
    t32 = t.astype(jnp.float32)
    x1, x2, rest = t32[..., :half], t32[..., half:ROPE_DIM], t32[..., ROPE_DIM:]
    rot = jnp.concatenate([x1 * cos - x2 * sin, x2 * cos + x1 * sin, rest], axis=-1)
    return rot.astype(t.dtype)


def swa_mixer(h, positions, w_in, b_in, q_gain, k_gain, sinks, w_out, b_out):
    B, S, _ = h.shape
    Hq, Hkv, G, hd, BLK = SWA_Q_HEADS, SWA_KV_HEADS, SWA_GROUP, SWA_HEAD_DIM, SWA_BLOCK
    proj = h @ w_in + b_in
    q, k, v = jnp.split(proj, [Hq * hd, (Hq + Hkv) * hd], axis=-1)
    q = rms_norm(q.reshape(B, S, Hq, hd), q_gain)
    k = rms_norm(k.reshape(B, S, Hkv, hd), k_gain)
    v = v.reshape(B, S, Hkv, hd)
    inv_freq = ROPE_THETA ** (-jnp.arange(0, ROPE_DIM, 2, dtype=jnp.float32) / ROPE_DIM)
    ang = positions.astype(jnp.float32)[..., None] * inv_freq
    cos, sin = jnp.cos(ang)[:, :, None, :], jnp.sin(ang)[:, :, None, :]
    q = apply_partial_rotary(q, cos, sin)
    k = apply_partial_rotary(k, cos, sin)

    nb = S // BLK
    qb = q.reshape(B, nb, BLK, Hkv, G, hd)

    def with_prev(t):
        t = t.reshape(B, nb, BLK, Hkv, hd)
        prev = jnp.pad(t[:, :-1], ((0, 0), (1, 0), (0, 0), (0, 0), (0, 0)))
        return jnp.concatenate([prev, t], axis=2)

    kk, vv = with_prev(k), with_prev(v)
    scores = jnp.einsum('bnqhgd,bnkhd->bhgnqk', qb, kk).astype(jnp.float32) * (hd ** -0.5)
    rel = jnp.arange(BLK)[:, None] + BLK - jnp.arange(2 * BLK)[None, :]
    band = (rel >= 0) & (rel < SWA_WINDOW)
    key_pos = jnp.arange(nb)[:, None] * BLK + jnp.arange(2 * BLK)[None, :] - BLK
    mask = band[None] & (key_pos >= 0)[:, None, :]
    scores = jnp.where(mask, scores, -jnp.inf)
    sink = sinks.astype(jnp.float32).reshape(Hkv, G)[None, :, :, None, None]
    m = jnp.maximum(scores.max(axis=-1), sink)
    p = jnp.exp(scores - m[..., None])
    denom = p.sum(axis=-1) + jnp.exp(sink - m)
    p = p / denom[..., None]
    o = jnp.einsum('bhgnqk,bnkhd->bnqhgd', p.astype(vv.dtype), vv).reshape(B, S, Hq * hd)
    return o @ w_out + b_out


def peer_ffn(h, w_q, sub_keys, u, v):
    B, S, D = h.shape
    T = B * S
    K, BLK = PEER_TOPK, PEER_TOKEN_BLOCK
    xt = h.reshape(T // BLK, BLK, D)

    def block(xb):
        q = (xb @ w_q).reshape(BLK, PEER_HEADS, PEER_QDIM)
        s1 = jnp.einsum('thd,hkd->thk', q[..., :PEER_HALF], sub_keys[:, 0]).astype(jnp.float32)
        s2 = jnp.einsum('thd,hkd->thk', q[..., PEER_HALF:], sub_keys[:, 1]).astype(jnp.float32)
        v1, i1 = lax.top_k(s1, K)
        v2, i2 = lax.top_k(s2, K)
        cand = (v1[..., :, None] + v2[..., None, :]).reshape(BLK, PEER_HEADS, K * K)
        sc, ci = lax.top_k(cand, K)
        e = jnp.take_along_axis(i1, ci // K, axis=-1) * PEER_N_KEYS + jnp.take_along_axis(i2, ci % K, axis=-1)
        g = jax.nn.softmax(sc, axis=-1)
        ue = u[e]
        ve = v[e]
        a = jax.nn.gelu(jnp.einsum('thkd,td->thk', ue, xb).astype(jnp.float32), approximate=False) * g
        return jnp.einsum('thk,thkd->td', a.astype(xb.dtype), ve)

    return lax.map(block, xt).reshape(B, S, D)


def setup_inputs(seed: int = 0) -> dict:
    key = jax.random.key(seed)
    ks = jax.random.split(key, 20)
    f32 = jnp.float32
    D = D_MODEL

    def w(k, shape, fan_in):
        return jax.random.normal(k, shape, f32) * (fan_in ** -0.5)

    def gain(k, shape):
        return 1.0 + 0.02 * jax.random.normal(k, shape, f32)

    return {
        "x": jax.random.normal(ks[0], (BATCH, SEQ, D), f32),
        "positions": jnp.broadcast_to(jnp.arange(SEQ, dtype=jnp.int32), (BATCH, SEQ)),
        "ln_mix": gain(ks[1], (DEPTH, D)),
        "ln_ffn": gain(ks[2], (DEPTH, D)),
        "gla_w_in": w(ks[3], (N_A_LAYERS, D, GLA_IN), D),
        "gla_w_alpha": w(ks[4], (N_A_LAYERS, GLA_GATE_RANK, GLA_HEADS * GLA_DK), GLA_GATE_RANK),
        "gla_b_alpha": 0.1 * jax.random.normal(ks[5], (N_A_LAYERS, GLA_HEADS * GLA_DK), f32),
        "gla_norm": gain(ks[6], (N_A_LAYERS, GLA_HEADS, GLA_DV)),
        "gla_w_out": w(ks[7], (N_A_LAYERS, GLA_HEADS * GLA_DV, D), GLA_HEADS * GLA_DV),
        "swa_w_in": w(ks[8], (N_B_LAYERS, D, SWA_IN), D),
        "swa_b_in": 0.02 * jax.random.normal(ks[9], (N_B_LAYERS, SWA_IN), f32),
        "swa_q_norm": gain(ks[10], (N_B_LAYERS, SWA_HEAD_DIM)),
        "swa_k_norm": gain(ks[11], (N_B_LAYERS, SWA_HEAD_DIM)),
        "swa_sinks": 0.5 * jax.random.normal(ks[12], (N_B_LAYERS, SWA_Q_HEADS), f32),
        "swa_w_out": w(ks[13], (N_B_LAYERS, SWA_Q_HEADS * SWA_HEAD_DIM, D), SWA_Q_HEADS * SWA_HEAD_DIM),
        "swa_b_out": 0.02 * jax.random.normal(ks[14], (N_B_LAYERS, D), f32),
        "peer_w_q": w(ks[15], (DEPTH, D, PEER_HEADS * PEER_QDIM), D),
        "peer_keys": w(ks[16], (DEPTH, PEER_HEADS, 2, PEER_N_KEYS, PEER_HALF), PEER_HALF),
        "peer_u": w(ks[17], (DEPTH, PEER_N_EXPERTS, D), D),
        "peer_v": w(ks[18], (DEPTH, PEER_N_EXPERTS, D), D),
    }


def reference(x, positions, ln_mix, ln_ffn, gla_w_in, gla_w_alpha, gla_b_alpha, gla_norm, gla_w_out,
              swa_w_in, swa_b_in, swa_q_norm, swa_k_norm, swa_sinks, swa_w_out, swa_b_out,
              peer_w_q, peer_keys, peer_u, peer_v):
    h = x
    for layer in range(DEPTH):
        j = layer // N_MIXERS
        hn = rms_norm(h, ln_mix[layer])
        if layer % N_MIXERS == 0:
            mix = gla_mixer(hn, gla_w_in[j], gla_w_alpha[j], gla_b_alpha[j], gla_norm[j], gla_w_out[j])
        else:
            mix = swa_mixer(hn, positions, swa_w_in[j], swa_b_in[j], swa_q_norm[j], swa_k_norm[j],
                            swa_sinks[j], swa_w_out[j], swa_b_out[j])
        h = h + mix
        h = h + peer_ffn(rms_norm(h, ln_ffn[layer]), peer_w_q[layer], peer_keys[layer], peer_u[layer], peer_v[layer])
    return h
```

```python
import functools

import jax
import jax.numpy as jnp
import numpy as np
from jax import lax
from jax.experimental import pallas as pl
from jax.experimental.pallas import tpu as pltpu

F32 = jnp.float32
BF16 = jnp.bfloat16

D_MODEL = 1024
RMS_EPS = 1e-6

GLA_HEADS = 4
GLA_DK = 128
GLA_DV = 256
GLA_RANK = 16
GLA_TAU = 16.0
GLA_CHUNK = 64

SWA_HD = 64
SWA_QH = 16
SWA_KVH = 2
SWA_GROUP = 8
SWA_BLK = 128
ROPE_THETA = 500000.0
ROPE_DIM = 16

PEER_HEADS = 8
PEER_KEYS = 128
PEER_EXPERTS = PEER_KEYS * PEER_KEYS
PEER_HALF = 64
PEER_TOPK = 16
PEER_SLOTS = PEER_HEADS * PEER_TOPK

LANES = 128
TOKEN_TILE = 256
PEER_TOKENS = 128
FOLD_STRIDE = 136
VMEM_LIMIT = 48 * 1024 * 1024


def _cparams(n_axes=1):
    return pltpu.CompilerParams(
        dimension_semantics=("arbitrary",) * n_axes, vmem_limit_bytes=VMEM_LIMIT)


def _rms(x, gain):
    ms = jnp.mean(x * x, axis=-1, keepdims=True)
    return x * lax.rsqrt(ms + RMS_EPS) * gain


def _dot(a, b):
    return jnp.dot(a, b, preferred_element_type=F32)


def _dot_nt(a, b):
    return lax.dot_general(a, b, (((1,), (1,)), ((), ())), preferred_element_type=F32)


def _dot_tn(a, b):
    return lax.dot_general(a, b, (((0,), (0,)), ((), ())), preferred_element_type=F32)


def _split2(x):
    hi = x.astype(BF16)
    lo = (x - hi.astype(F32)).astype(BF16)
    return hi, lo


def _split3(x):
    a = x.astype(BF16)
    r = x - a.astype(F32)
    b = r.astype(BF16)
    c = (r - b.astype(F32)).astype(BF16)
    return a, b, c


def _gla_in_kernel(x_ref, gain_ref, w_ref, wlr_ref, wa_ref, ba_ref,
                   q_ref, k_ref, v_ref, r_ref, g_ref):
    hk = GLA_HEADS * GLA_DK
    hv = GLA_HEADS * GLA_DV
    xn = _rms(x_ref[...], gain_ref[...]).astype(BF16)
    q_ref[...] = _dot(xn, w_ref[:, 0:hk])
    k_ref[...] = _dot(xn, w_ref[:, hk:2 * hk])
    v_ref[...] = _dot(xn, w_ref[:, 2 * hk:2 * hk + hv]).astype(BF16)
    r_ref[...] = _dot(xn, w_ref[:, 2 * hk + hv:2 * hk + 2 * hv])
    lr = _dot(xn, wlr_ref[...])
    z = _dot(lr.astype(BF16), wa_ref[...]) + ba_ref[...]
    g_ref[...] = (jnp.minimum(z, 0.0) - jnp.log(1.0 + jnp.exp(-jnp.abs(z)))) / GLA_TAU


def _gla_in(x2, gain, w_main, w_lr, w_alpha, b_alpha):
    T = x2.shape[0]
    hk = GLA_HEADS * GLA_DK
    hv = GLA_HEADS * GLA_DV
    tm = TOKEN_TILE
    row = lambda n: pl.BlockSpec((tm, n), lambda i: (i, 0))
    full = lambda a: pl.BlockSpec(a.shape, lambda i: (0,) * a.ndim)
    return pl.pallas_call(
        _gla_in_kernel,
        grid=(T // tm,),
        in_specs=[row(D_MODEL), full(gain), full(w_main), full(w_lr), full(w_alpha), full(b_alpha)],
        out_specs=[row(hk), row(hk), row(hv), row(hv), row(hk)],
        out_shape=[jax.ShapeDtypeStruct((T, hk), F32), jax.ShapeDtypeStruct((T, hk), F32),
                   jax.ShapeDtypeStruct((T, hv), BF16), jax.ShapeDtypeStruct((T, hv), F32),
                   jax.ShapeDtypeStruct((T, hk), F32)],
        compiler_params=_cparams(),
        name="gla_in",
    )(x2, gain, w_main, w_lr, w_alpha, b_alpha)


def _gla_scan_kernel(q_ref, k_ref, g_ref, v_ref, r_ref, gain_ref, o_ref, st_ref):
    C = GLA_CHUNK
    nb = q_ref.shape[0]

    @pl.when(pl.program_id(0) == 0)
    def _():
        st_ref[...] = jnp.zeros_like(st_ref)

    ri = lax.broadcasted_iota(jnp.int32, (C, C), 0)
    ci = lax.broadcasted_iota(jnp.int32, (C, C), 1)
    tri = ri >= ci
    tri_b = jnp.where(tri, 1.0, 0.0).astype(BF16)
    for b in range(nb):
        g1, g2, g3 = _split3(g_ref[b])
        bc = _dot(tri_b, g1) + _dot(tri_b, g2) + _dot(tri_b, g3)
        eb = jnp.exp(bc)
        enb = jnp.exp(-bc)
        ebl = jnp.exp(bc[C - 1:C, :])
        qd = q_ref[b] * (GLA_DK ** -0.5) * eb
        ki = k_ref[b] * enb
        kf = ki * ebl
        for h in range(GLA_HEADS):
            ks = slice(h * GLA_DK, (h + 1) * GLA_DK)
            vs = slice(h * GLA_DV, (h + 1) * GLA_DV)
            qd_h = qd[:, ks].astype(BF16)
            ki_h = ki[:, ks].astype(BF16)
            kf_h = kf[:, ks].astype(BF16)
            v_h = v_ref[b, :, vs]
            att = jnp.where(tri, _dot_nt(qd_h, ki_h), 0.0).astype(BF16)
            st = st_ref[b * GLA_HEADS + h]
            o = _dot(att, v_h) + _dot_nt(qd_h, st.astype(BF16))
            st_ref[b * GLA_HEADS + h] = st * ebl[:, ks] + _dot_tn(v_h, kf_h)
            ms = jnp.mean(o * o, axis=-1, keepdims=True)
            on = o * lax.rsqrt(ms + RMS_EPS) * gain_ref[:, vs]
            rr = r_ref[b, :, vs]
            o_ref[b, :, vs] = (on * (rr / (1.0 + jnp.exp(-rr)))).astype(BF16)


def _gla_scan(q, k, g, v, r, gain):
    B, S, hk = q.shape
    hv = v.shape[2]
    C = GLA_CHUNK
    blk = lambda n: pl.BlockSpec((B, C, n), lambda c: (0, c, 0))
    return pl.pallas_call(
        _gla_scan_kernel,
        grid=(S // C,),
        in_specs=[blk(hk), blk(hk), blk(hk), blk(hv), blk(hv),
                  pl.BlockSpec(gain.shape, lambda c: (0, 0))],
        out_specs=blk(hv),
        out_shape=jax.ShapeDtypeStruct((B, S, hv), BF16),
        scratch_shapes=[pltpu.VMEM((B * GLA_HEADS, GLA_DV, GLA_DK), F32)],
        compiler_params=_cparams(),
        name="gla_scan",
    )(q, k, g, v, r, gain)


def _proj_res_kernel(a_ref, w_ref, res_ref, o_ref):
    o_ref[...] = res_ref[...] + _dot(a_ref[...], w_ref[...])


def _proj_bias_res_kernel(a_ref, w_ref, b_ref, res_ref, o_ref):
    o_ref[...] = res_ref[...] + (_dot(a_ref[...], w_ref[...]) + b_ref[...])


def _proj_res(a, w, bias, res, name):
    T, K = a.shape
    N = w.shape[1]
    tm = TOKEN_TILE
    row = lambda n: pl.BlockSpec((tm, n), lambda i: (i, 0))
    wspec = pl.BlockSpec((K, N), lambda i: (0, 0))
    if bias is None:
        body, specs, args = _proj_res_kernel, [row(K), wspec, row(N)], (a, w, res)
    else:
        body = _proj_bias_res_kernel
        specs = [row(K), wspec, pl.BlockSpec((1, N), lambda i: (0, 0)), row(N)]
        args = (a, w, bias, res)
    return pl.pallas_call(
        body,
        grid=(T // tm,),
        in_specs=specs,
        out_specs=row(N),
        out_shape=jax.ShapeDtypeStruct((T, N), F32),
        compiler_params=_cparams(),
        name=name,
    )(*args)


def _swa_in_kernel(h_ref, pos_ref, gain_ref, w_ref, b_ref, pq_ref, pk_ref, qg_ref, kg_ref,
                   freq_ref, q_ref, k_ref, v_ref):
    nq = SWA_QH * SWA_HD
    nk = 2 * SWA_KVH * SWA_HD
    xn = _rms(h_ref[...], gain_ref[...]).astype(BF16)
    proj = _dot(xn, w_ref[...]) + b_ref[...]
    q = proj[:, 0:nq]
    k = proj[:, nq:nq + nk]
    v = proj[:, nq + nk:nq + 2 * nk]

    def headnorm(t, p_ref, gain):
        s1, s2 = _split2(t * t)
        ms = _dot(s1, p_ref[...]) + _dot(s2, p_ref[...])
        return t * lax.rsqrt(ms + RMS_EPS) * gain

    q = headnorm(q, pq_ref, qg_ref[...])
    k = headnorm(k, pk_ref, kg_ref[...])

    ang = pos_ref[...].astype(F32) * freq_ref[...]
    cs = jnp.cos(ang)
    sn = jnp.sin(ang)
    hl = lax.broadcasted_iota(jnp.int32, ang.shape, 1) & (SWA_HD - 1)
    half = ROPE_DIM // 2
    s_lo = jnp.where(hl < half, -sn, 0.0)
    s_hi = jnp.where((hl >= half) & (hl < ROPE_DIM), sn, 0.0)

    def rope(t):
        outs = []
        for c in range(t.shape[1] // LANES):
            x = t[:, c * LANES:(c + 1) * LANES]
            x_up = pltpu.roll(x, LANES - half, axis=1)
            x_dn = pltpu.roll(x, half, axis=1)
            outs.append(x * cs + x_up * s_lo + x_dn * s_hi)
        return jnp.concatenate(outs, axis=1)

    q_ref[...] = (rope(q) * (SWA_HD ** -0.5)).astype(BF16)
    k_ref[...] = rope(k).astype(BF16)
    v_ref[...] = v.astype(BF16)


def _swa_in(h2, pos, gain, w, b, pq, pk, qg, kg, freq):
    T = h2.shape[0]
    nq = SWA_QH * SWA_HD
    nk = 2 * SWA_KVH * SWA_HD
    tm = TOKEN_TILE
    row = lambda n: pl.BlockSpec((tm, n), lambda i: (i, 0))
    full = lambda a: pl.BlockSpec(a.shape, lambda i: (0,) * a.ndim)
    return pl.pallas_call(
        _swa_in_kernel,
        grid=(T // tm,),
        in_specs=[row(D_MODEL), row(1), full(gain), full(w), full(b), full(pq), full(pk),
                  full(qg), full(kg), full(freq)],
        out_specs=[row(nq), row(nk), row(nk)],
        out_shape=[jax.ShapeDtypeStruct((T, nq), BF16), jax.ShapeDtypeStruct((T, nk), BF16),
                   jax.ShapeDtypeStruct((T, nk), BF16)],
        compiler_params=_cparams(),
        name="swa_in",
    )(h2, pos, gain, w, b, pq, pk, qg, kg, freq)


def _swa_attn_kernel(sink_ref, q_ref, kc_ref, kp_ref, vc_ref, vp_ref, o_ref):
    n = pl.program_id(1)
    blk = SWA_BLK
    kk = jnp.concatenate([kp_ref[...], kc_ref[...]], axis=0)
    vv = jnp.concatenate([vp_ref[...], vc_ref[...]], axis=0)
    qi = lax.broadcasted_iota(jnp.int32, (blk, 2 * blk), 0)
    ki = lax.broadcasted_iota(jnp.int32, (blk, 2 * blk), 1)
    rel = qi + blk - ki
    first_key = jnp.where(n > 0, 0, blk)
    mask = (rel >= 0) & (rel < blk) & (ki >= first_key)
    lane = lax.broadcasted_iota(jnp.int32, (blk, LANES), 1)
    first = lane < SWA_HD
    for j in range(SWA_KVH):
        k2 = kk[:, j * LANES:(j + 1) * LANES]
        v2 = vv[:, j * LANES:(j + 1) * LANES]
        for gp in range(SWA_GROUP // 2):
            c0 = j * (SWA_GROUP * SWA_HD) + gp * LANES
            qp = q_ref[:, c0:c0 + LANES]
            outs = []
            for hf in range(2):
                qm = jnp.where(first if hf == 0 else ~first, qp, jnp.zeros_like(qp))
                s = jnp.where(mask, _dot_nt(qm, k2), -jnp.inf)
                sink = sink_ref[j * SWA_GROUP + 2 * gp + hf]
                m = jnp.maximum(jnp.max(s, axis=-1, keepdims=True), sink)
                p = jnp.exp(s - m)
                denom = jnp.sum(p, axis=-1, keepdims=True) + jnp.exp(sink - m)
                outs.append(_dot((p / denom).astype(BF16), v2))
            o_ref[:, c0:c0 + LANES] = jnp.where(first, outs[0], outs[1]).astype(BF16)


def _swa_attn(sinks, q, k2, v2):
    B, S, nq = q.shape
    nk = k2.shape[2]
    blk = SWA_BLK
    cur = lambda n_: pl.BlockSpec((None, blk, n_), lambda b, n: (b, n, 0))
    prev = lambda n_: pl.BlockSpec((None, blk, n_), lambda b, n: (b, jnp.maximum(n - 1, 0), 0))
    return pl.pallas_call(
        _swa_attn_kernel,
        grid=(B, S // blk),
        in_specs=[pl.BlockSpec(memory_space=pltpu.SMEM), cur(nq), cur(nk), prev(nk), cur(nk), prev(nk)],
        out_specs=cur(nq),
        out_shape=jax.ShapeDtypeStruct((B, S, nq), BF16),
        compiler_params=_cparams(2),
        name="swa_attn",
    )(sinks, q, k2, k2, v2, v2)


def _top16_rows(a, nrows):
    rid = lax.broadcasted_iota(jnp.int32, a.shape, 0)
    vals, ids = [], []
    for _ in range(PEER_TOPK):
        m = jnp.max(a, axis=0, keepdims=True)
        am = jnp.min(jnp.where(a == m, rid, nrows), axis=0, keepdims=True)
        vals.append(m)
        ids.append(am)
        a = jnp.where(rid == am, -jnp.inf, a)
    return jnp.concatenate(vals, axis=0), jnp.concatenate(ids, axis=0)


def _staircase_blocks():
    K = PEER_TOPK
    blocks = []
    for i in range(4):
        nj = K // (i + 1)
        for j0 in range(0, nj, 8):
            blocks.append(("j", i, j0, [j0 + r < nj for r in range(8)]))
    for j in range(3):
        for i0 in (0, 8):
            keep = [(i0 + r >= 4) and ((i0 + r + 1) * (j + 1) <= K) for r in range(8)]
            if any(keep):
                blocks.append(("i", j, i0, keep))
    return blocks


_STAIR = _staircase_blocks()


def _peer_select_kernel(h_ref, gain_ref, wq_ref, kbd_ref, xn_ref, idx_ref, gate_ref):
    K = PEER_TOPK
    tm = h_ref.shape[0]
    xn = _rms(h_ref[...], gain_ref[...])
    xn_ref[...] = xn
    qv = _dot(xn.astype(BF16), wq_ref[...]).astype(BF16)
    st = _dot_nt(kbd_ref[...], qv)
    sub = lax.broadcasted_iota(jnp.int32, (8, tm), 0)
    for h in range(PEER_HEADS):
        r0 = 2 * h * PEER_KEYS
        v1, i1 = _top16_rows(st[r0:r0 + PEER_KEYS], PEER_KEYS)
        v2, i2 = _top16_rows(st[r0 + PEER_KEYS:r0 + 2 * PEER_KEYS], PEER_KEYS)
        cv, cc = [], []
        for kind, fixed, start, keep in _STAIR:
            keepm = functools.reduce(jnp.logical_or, [sub == r for r in range(8) if keep[r]])
            if kind == "j":
                val = v1[fixed:fixed + 1] + v2[start:start + 8]
                code = fixed * K + start + sub
            else:
                val = v1[start:start + 8] + v2[fixed:fixed + 1]
                code = (start + sub) * K + fixed
            cv.append(jnp.where(keepm, val, -jnp.inf))
            cc.append(code)
        cand = jnp.concatenate(cv, axis=0)
        code = jnp.concatenate(cc, axis=0)
        scs, cis = [], []
        for _ in range(K):
            m = jnp.max(cand, axis=0, keepdims=True)
            cm = jnp.min(jnp.where(cand == m, code, K * K), axis=0, keepdims=True)
            scs.append(m)
            cis.append(cm)
            cand = jnp.where(code == cm, -jnp.inf, cand)
        sc = jnp.concatenate(scs, axis=0)
        ci = jnp.concatenate(cis, axis=0)
        chi = ci >> 4
        clo = ci & (K - 1)
        e1 = jnp.zeros_like(ci)
        e2 = jnp.zeros_like(ci)
        for i in range(K):
            e1 = jnp.where(chi == i, i1[i:i + 1], e1)
            e2 = jnp.where(clo == i, i2[i:i + 1], e2)
        idx_ref[h * K:(h + 1) * K, :] = e1 * PEER_KEYS + e2
        ex = jnp.exp(sc - sc[0:1])
        gate_ref[h * K:(h + 1) * K, :] = ex / jnp.sum(ex, axis=0, keepdims=True)


def _peer_select(h2, gain, wq, kbd):
    T = h2.shape[0]
    tm = TOKEN_TILE
    full = lambda a: pl.BlockSpec(a.shape, lambda i: (0,) * a.ndim)
    return pl.pallas_call(
        _peer_select_kernel,
        grid=(T // tm,),
        in_specs=[pl.BlockSpec((tm, D_MODEL), lambda i: (i, 0)), full(gain), full(wq), full(kbd)],
        out_specs=[pl.BlockSpec((tm, D_MODEL), lambda i: (i, 0)),
                   pl.BlockSpec((PEER_SLOTS, tm), lambda i: (0, i)),
                   pl.BlockSpec((PEER_SLOTS, tm), lambda i: (0, i))],
        out_shape=[jax.ShapeDtypeStruct((T, D_MODEL), F32),
                   jax.ShapeDtypeStruct((PEER_SLOTS, T), jnp.int32),
                   jax.ShapeDtypeStruct((PEER_SLOTS, T), F32)],
        compiler_params=_cparams(),
        name="peer_select",
    )(h2, gain, wq, kbd)


def _pack_table(w):
    wb = w.astype(BF16)
    half = D_MODEL // 2
    lo = lax.bitcast_convert_type(wb[:, :half], jnp.uint16).astype(jnp.uint32)
    hi = lax.bitcast_convert_type(wb[:, half:], jnp.uint16).astype(jnp.uint32)
    return (lo | (hi << 16)).reshape(w.shape[0], 4, LANES)


def _unpack(w):
    lo = lax.bitcast_convert_type(w << 16, F32)
    hi = lax.bitcast_convert_type(w & jnp.uint32(0xFFFF0000), F32)
    return lo, hi


def _peer_u_kernel(idx_ref, xlo_ref, xhi_ref, gate_ref, tab_ref, a_ref, s0_ref, s1_ref):
    tb = PEER_TOKENS
    ss = FOLD_STRIDE
    lane = lax.broadcasted_iota(jnp.int32, (8, tb), 1)

    def gather(t, s_ref):
        xlo = xlo_ref[t]
        xhi = xhi_ref[t]
        for k in range(PEER_SLOTS):
            lo, hi = _unpack(tab_ref[idx_ref[k, t]])
            s_ref[pl.ds(k, 4, stride=ss), :] = lo * xlo + hi * xhi

    def fold(t, s_ref):
        sel = lane == t
        for j in range(PEER_SLOTS // 8):
            r = s_ref[pl.ds(8 * j, 8), :]
            for s in range(1, 4):
                r = r + s_ref[pl.ds(s * ss + 8 * j, 8), :]
            tot = jnp.sum(r, axis=1, keepdims=True)
            a_ref[pl.ds(8 * j, 8), :] = jnp.where(sel, tot, a_ref[pl.ds(8 * j, 8), :])

    a_ref[...] = jnp.zeros_like(a_ref)
    gather(0, s0_ref)

    def pair(i, carry):
        t = 2 * i
        gather(t + 1, s1_ref)
        fold(t, s0_ref)
        gather(jnp.minimum(t + 2, tb - 1), s0_ref)
        fold(t + 1, s1_ref)
        return carry

    lax.fori_loop(0, tb // 2, pair, 0)
    z = a_ref[...]
    a_ref[...] = 0.5 * z * (1.0 + lax.erf(z * (2.0 ** -0.5))) * gate_ref[...]


def _peer_u(idx_t, xlo, xhi, gate_t, tab):
    T = idx_t.shape[1]
    tb = PEER_TOKENS
    ne = tab.shape[0]
    return pl.pallas_call(
        _peer_u_kernel,
        grid=(T // tb,),
        in_specs=[
            pl.BlockSpec((PEER_SLOTS, tb), lambda i: (0, i), memory_space=pltpu.SMEM),
            pl.BlockSpec((tb, 4, LANES), lambda i: (i, 0, 0)),
            pl.BlockSpec((tb, 4, LANES), lambda i: (i, 0, 0)),
            pl.BlockSpec((PEER_SLOTS, tb), lambda i: (0, i)),
            pl.BlockSpec((ne, 4, LANES), lambda i: (0, 0, 0), pipeline_mode=pl.Buffered(1)),
        ],
        out_specs=pl.BlockSpec((PEER_SLOTS, tb), lambda i: (0, i)),
        out_shape=jax.ShapeDtypeStruct((PEER_SLOTS, T), F32),
        scratch_shapes=[pltpu.VMEM((4 * FOLD_STRIDE, LANES), F32),
                        pltpu.VMEM((4 * FOLD_STRIDE, LANES), F32)],
        compiler_params=_cparams(),
        name="peer_u",
    )(idx_t, xlo, xhi, gate_t, tab)


def _peer_v_kernel(idx_ref, a_ref, rlo_ref, rhi_ref, tab_ref, olo_ref, ohi_ref, b0_ref, b1_ref):
    tb = PEER_TOKENS
    lane = lax.broadcasted_iota(jnp.int32, (PEER_SLOTS, tb), 1)

    def prep(t, b_ref):
        col = jnp.sum(jnp.where(lane == t, a_ref[...], 0.0), axis=1, keepdims=True)
        b_ref[...] = jnp.broadcast_to(col, (PEER_SLOTS, LANES))

    def accum(t, b_ref):
        acc_lo = [rlo_ref[t], jnp.zeros((4, LANES), F32)]
        acc_hi = [rhi_ref[t], jnp.zeros((4, LANES), F32)]
        for k in range(PEER_SLOTS):
            lo, hi = _unpack(tab_ref[idx_ref[k, t]])
            ab = jnp.broadcast_to(b_ref[pl.ds(k, 1), :], (4, LANES))
            acc_lo[k % 2] = acc_lo[k % 2] + ab * lo
            acc_hi[k % 2] = acc_hi[k % 2] + ab * hi
        olo_ref[t] = acc_lo[0] + acc_lo[1]
        ohi_ref[t] = acc_hi[0] + acc_hi[1]

    prep(0, b0_ref)

    def pair(i, carry):
        t = 2 * i
        prep(t + 1, b1_ref)
        accum(t, b0_ref)
        prep(jnp.minimum(t + 2, tb - 1), b0_ref)
        accum(t + 1, b1_ref)
        return carry

    lax.fori_loop(0, tb // 2, pair, 0)


def _peer_v(idx_t, a_t, rlo, rhi, tab):
    T = idx_t.shape[1]
    tb = PEER_TOKENS
    ne = tab.shape[0]
    tok = pl.BlockSpec((tb, 4, LANES), lambda i: (i, 0, 0))
    return pl.pallas_call(
        _peer_v_kernel,
        grid=(T // tb,),
        in_specs=[
            pl.BlockSpec((PEER_SLOTS, tb), lambda i: (0, i), memory_space=pltpu.SMEM),
            pl.BlockSpec((PEER_SLOTS, tb), lambda i: (0, i)),
            tok, tok,
            pl.BlockSpec((ne, 4, LANES), lambda i: (0, 0, 0), pipeline_mode=pl.Buffered(1)),
        ],
        out_specs=[tok, tok],
        out_shape=[jax.ShapeDtypeStruct((T, 4, LANES), F32), jax.ShapeDtypeStruct((T, 4, LANES), F32)],
        scratch_shapes=[pltpu.VMEM((PEER_SLOTS, LANES), F32), pltpu.VMEM((PEER_SLOTS, LANES), F32)],
        compiler_params=_cparams(),
        name="peer_v",
    )(idx_t, a_t, rlo, rhi, tab)


def _halves(a2):
    a4 = a2.reshape(a2.shape[0], 2, 4, LANES)
    return a4[:, 0], a4[:, 1]


def _peer_layer(h2, gain, wq, keys, u, v):
    T = h2.shape[0]
    eye = jnp.eye(PEER_HEADS * 2, dtype=F32)
    kbd = jnp.einsum("pkd,pq->pkqd", keys.reshape(PEER_HEADS * 2, PEER_KEYS, PEER_HALF), eye)
    kbd = kbd.reshape(PEER_HEADS * 2 * PEER_KEYS, D_MODEL).astype(BF16)
    xn, idx_t, gate_t = _peer_select(h2, gain.reshape(1, D_MODEL), wq.astype(BF16), kbd)
    xlo, xhi = _halves(xn)
    a_t = _peer_u(idx_t, xlo, xhi, gate_t, _pack_table(u))
    rlo, rhi = _halves(h2)
    olo, ohi = _peer_v(idx_t, a_t, rlo, rhi, _pack_table(v))
    return jnp.concatenate([olo.reshape(T, D_MODEL // 2), ohi.reshape(T, D_MODEL // 2)], axis=1)


def _gla_layer(x2, B, S, gain, w_in, w_alpha, b_alpha, norm_gain, w_out):
    hk = GLA_HEADS * GLA_DK
    hv = GLA_HEADS * GLA_DV
    w_main = w_in[:, :2 * hk + 2 * hv].astype(BF16)
    w_lr = jnp.pad(w_in[:, 2 * hk + 2 * hv:], ((0, 0), (0, LANES - GLA_RANK))).astype(BF16)
    wa = jnp.pad(w_alpha, ((0, LANES - GLA_RANK), (0, 0))).astype(BF16)
    q, k, v, r, g = _gla_in(x2, gain.reshape(1, D_MODEL), w_main, w_lr, wa, b_alpha.reshape(1, hk))
    sh = lambda a: a.reshape(B, S, a.shape[1])
    og = _gla_scan(sh(q), sh(k), sh(g), sh(v), sh(r), norm_gain.reshape(1, hv))
    return _proj_res(og.reshape(B * S, hv), w_out.astype(BF16), None, x2, "gla_out")


def _swa_layer(h2, pos, B, S, gain, w_in, b_in, q_gain, k_gain, sinks, w_out, b_out):
    nq = SWA_QH * SWA_HD
    hd = SWA_HD
    wq_, wk_, wv_ = w_in[:, :nq], w_in[:, nq:nq + SWA_KVH * hd], w_in[:, nq + SWA_KVH * hd:]
    bq_, bk_, bv_ = b_in[:nq], b_in[nq:nq + SWA_KVH * hd], b_in[nq + SWA_KVH * hd:]

    def dup(a):
        parts = []
        for j in range(SWA_KVH):
            parts += [a[..., j * hd:(j + 1) * hd]] * 2
        return jnp.concatenate(parts, axis=-1)

    w = jnp.concatenate([wq_, dup(wk_), dup(wv_)], axis=1).astype(BF16)
    b = jnp.concatenate([bq_, dup(bk_), dup(bv_)]).reshape(1, -1)
    nk = 2 * SWA_KVH * hd
    pmat = lambda n: (jnp.kron(jnp.eye(n // hd, dtype=F32), jnp.ones((hd, hd), F32)) / hd).astype(BF16)
    inv_freq = ROPE_THETA ** (-jnp.arange(0, ROPE_DIM, 2, dtype=F32) / ROPE_DIM)
    hl = np.arange(LANES) % hd
    freq = jnp.where(hl < ROPE_DIM, inv_freq[hl % (ROPE_DIM // 2)], 0.0).reshape(1, LANES)
    q, k2, v2 = _swa_in(h2, pos.reshape(B * S, 1), gain.reshape(1, D_MODEL), w, b, pmat(nq), pmat(nk),
                        jnp.tile(q_gain, nq // hd).reshape(1, nq), jnp.tile(k_gain, nk // hd).reshape(1, nk), freq)
    o = _swa_attn(sinks, q.reshape(B, S, nq), k2.reshape(B, S, nk), v2.reshape(B, S, nk))
    return _proj_res(o.reshape(B * S, nq), w_out.astype(BF16), b_out.reshape(1, D_MODEL), h2, "swa_out")


def kernel(x, positions, ln_mix, ln_ffn, gla_w_in, gla_w_alpha, gla_b_alpha, gla_norm, gla_w_out, swa_w_in, swa_b_in, swa_q_norm, swa_k_norm, swa_sinks, swa_w_out, swa_b_out, peer_w_q, peer_keys, peer_u, peer_v):
    B, S, D = x.shape
    h = x.reshape(B * S, D)
    h = _gla_layer(h, B, S, ln_mix[0], gla_w_in[0], gla_w_alpha[0], gla_b_alpha[0], gla_norm[0], gla_w_out[0])
    h = _peer_layer(h, ln_ffn[0], peer_w_q[0], peer_keys[0], peer_u[0], peer_v[0])
    h = _swa_layer(h, positions, B, S, ln_mix[1], swa_w_in[0], swa_b_in[0], swa_q_norm[0], swa_k_norm[0],
                   swa_sinks[0], swa_w_out[0], swa_b_out[0])
    h = _peer_layer(h, ln_ffn[1], peer_w_q[1], peer_keys[1], peer_u[1], peer_v[1])
    return h.reshape(B, S, D)
```

```python
import functools

import jax
import jax.numpy as jnp
import numpy as np
from jax import lax
from jax.experimental import pallas as pl
from jax.experimental.pallas import tpu as pltpu

F32 = jnp.float32
BF16 = jnp.bfloat16

D_MODEL = 1024
RMS_EPS = 1e-6

GLA_HEADS = 4
GLA_DK = 128
GLA_DV = 256
GLA_RANK = 16
GLA_TAU = 16.0
GLA_CHUNK = 64

SWA_HD = 64
SWA_QH = 16
SWA_KVH = 2
SWA_GROUP = 8
SWA_BLK = 128
ROPE_THETA = 500000.0
ROPE_DIM = 16

PEER_HEADS = 8
PEER_KEYS = 128
PEER_EXPERTS = PEER_KEYS * PEER_KEYS
PEER_HALF = 64
PEER_TOPK = 16
PEER_SLOTS = PEER_HEADS * PEER_TOPK

LANES = 128
TOKEN_TILE = 256
PEER_TOKENS = 128
IDX_GROUP = 8
FOLD_STRIDE = 136
VMEM_LIMIT = 48 * 1024 * 1024


def _cparams(n_axes=1):
    return pltpu.CompilerParams(
        dimension_semantics=("arbitrary",) * n_axes, vmem_limit_bytes=VMEM_LIMIT)


def _rms(x, gain):
    ms = jnp.mean(x * x, axis=-1, keepdims=True)
    return x * lax.rsqrt(ms + RMS_EPS) * gain


def _dot(a, b):
    return jnp.dot(a, b, preferred_element_type=F32)


def _dot_nt(a, b):
    return lax.dot_general(a, b, (((1,), (1,)), ((), ())), preferred_element_type=F32)


def _dot_tn(a, b):
    return lax.dot_general(a, b, (((0,), (0,)), ((), ())), preferred_element_type=F32)


def _split2(x):
    hi = x.astype(BF16)
    lo = (x - hi.astype(F32)).astype(BF16)
    return hi, lo


def _split3(x):
    a = x.astype(BF16)
    r = x - a.astype(F32)
    b = r.astype(BF16)
    c = (r - b.astype(F32)).astype(BF16)
    return a, b, c


def _gla_in_kernel(x_ref, gain_ref, w_ref, wlr_ref, wa_ref, ba_ref,
                   q_ref, k_ref, v_ref, r_ref, g_ref):
    hk = GLA_HEADS * GLA_DK
    hv = GLA_HEADS * GLA_DV
    xn = _rms(x_ref[...], gain_ref[...]).astype(BF16)
    q_ref[...] = _dot(xn, w_ref[:, 0:hk])
    k_ref[...] = _dot(xn, w_ref[:, hk:2 * hk])
    v_ref[...] = _dot(xn, w_ref[:, 2 * hk:2 * hk + hv]).astype(BF16)
    r_ref[...] = _dot(xn, w_ref[:, 2 * hk + hv:2 * hk + 2 * hv])
    lr = _dot(xn, wlr_ref[...])
    z = _dot(lr.astype(BF16), wa_ref[...]) + ba_ref[...]
    g_ref[...] = (jnp.minimum(z, 0.0) - jnp.log(1.0 + jnp.exp(-jnp.abs(z)))) / GLA_TAU


def _gla_in(x2, gain, w_main, w_lr, w_alpha, b_alpha):
    T = x2.shape[0]
    hk = GLA_HEADS * GLA_DK
    hv = GLA_HEADS * GLA_DV
    tm = TOKEN_TILE
    row = lambda n: pl.BlockSpec((tm, n), lambda i: (i, 0))
    full = lambda a: pl.BlockSpec(a.shape, lambda i: (0,) * a.ndim)
    return pl.pallas_call(
        _gla_in_kernel,
        grid=(T // tm,),
        in_specs=[row(D_MODEL), full(gain), full(w_main), full(w_lr), full(w_alpha), full(b_alpha)],
        out_specs=[row(hk), row(hk), row(hv), row(hv), row(hk)],
        out_shape=[jax.ShapeDtypeStruct((T, hk), F32), jax.ShapeDtypeStruct((T, hk), F32),
                   jax.ShapeDtypeStruct((T, hv), BF16), jax.ShapeDtypeStruct((T, hv), F32),
                   jax.ShapeDtypeStruct((T, hk), F32)],
        compiler_params=_cparams(),
        name="gla_in",
    )(x2, gain, w_main, w_lr, w_alpha, b_alpha)


def _gla_scan_kernel(q_ref, k_ref, g_ref, v_ref, r_ref, gain_ref, o_ref, st_ref):
    C = GLA_CHUNK
    nb = q_ref.shape[0]

    @pl.when(pl.program_id(0) == 0)
    def _():
        st_ref[...] = jnp.zeros_like(st_ref)

    ri = lax.broadcasted_iota(jnp.int32, (C, C), 0)
    ci = lax.broadcasted_iota(jnp.int32, (C, C), 1)
    tri = ri >= ci
    tri_b = jnp.where(tri, 1.0, 0.0).astype(BF16)
    for b in range(nb):
        g1, g2, g3 = _split3(g_ref[b])
        bc = _dot(tri_b, g1) + _dot(tri_b, g2) + _dot(tri_b, g3)
        eb = jnp.exp(bc)
        enb = jnp.exp(-bc)
        ebl = jnp.exp(bc[C - 1:C, :])
        qd = q_ref[b] * (GLA_DK ** -0.5) * eb
        ki = k_ref[b] * enb
        kf = ki * ebl
        for h in range(GLA_HEADS):
            ks = slice(h * GLA_DK, (h + 1) * GLA_DK)
            vs = slice(h * GLA_DV, (h + 1) * GLA_DV)
            qd_h = qd[:, ks].astype(BF16)
            ki_h = ki[:, ks].astype(BF16)
            kf_h = kf[:, ks].astype(BF16)
            v_h = v_ref[b, :, vs]
            att = jnp.where(tri, _dot_nt(qd_h, ki_h), 0.0).astype(BF16)
            st = st_ref[b * GLA_HEADS + h]
            o = _dot(att, v_h) + _dot_nt(qd_h, st.astype(BF16))
            st_ref[b * GLA_HEADS + h] = st * ebl[:, ks] + _dot_tn(v_h, kf_h)
            ms = jnp.mean(o * o, axis=-1, keepdims=True)
            on = o * lax.rsqrt(ms + RMS_EPS) * gain_ref[:, vs]
            rr = r_ref[b, :, vs]
            o_ref[b, :, vs] = (on * (rr / (1.0 + jnp.exp(-rr)))).astype(BF16)


def _gla_scan(q, k, g, v, r, gain):
    B, S, hk = q.shape
    hv = v.shape[2]
    C = GLA_CHUNK
    blk = lambda n: pl.BlockSpec((B, C, n), lambda c: (0, c, 0))
    return pl.pallas_call(
        _gla_scan_kernel,
        grid=(S // C,),
        in_specs=[blk(hk), blk(hk), blk(hk), blk(hv), blk(hv),
                  pl.BlockSpec(gain.shape, lambda c: (0, 0))],
        out_specs=blk(hv),
        out_shape=jax.ShapeDtypeStruct((B, S, hv), BF16),
        scratch_shapes=[pltpu.VMEM((B * GLA_HEADS, GLA_DV, GLA_DK), F32)],
        compiler_params=_cparams(),
        name="gla_scan",
    )(q, k, g, v, r, gain)


def _proj_res_kernel(a_ref, w_ref, res_ref, o_ref):
    o_ref[...] = res_ref[...] + _dot(a_ref[...], w_ref[...])


def _proj_bias_res_kernel(a_ref, w_ref, b_ref, res_ref, o_ref):
    o_ref[...] = res_ref[...] + (_dot(a_ref[...], w_ref[...]) + b_ref[...])


def _proj_res(a, w, bias, res, name):
    T, K = a.shape
    N = w.shape[1]
    tm = TOKEN_TILE
    row = lambda n: pl.BlockSpec((tm, n), lambda i: (i, 0))
    wspec = pl.BlockSpec((K, N), lambda i: (0, 0))
    if bias is None:
        body, specs, args = _proj_res_kernel, [row(K), wspec, row(N)], (a, w, res)
    else:
        body = _proj_bias_res_kernel
        specs = [row(K), wspec, pl.BlockSpec((1, N), lambda i: (0, 0)), row(N)]
        args = (a, w, bias, res)
    return pl.pallas_call(
        body,
        grid=(T // tm,),
        in_specs=specs,
        out_specs=row(N),
        out_shape=jax.ShapeDtypeStruct((T, N), F32),
        compiler_params=_cparams(),
        name=name,
    )(*args)


def _swa_in_kernel(h_ref, pos_ref, gain_ref, w_ref, b_ref, pq_ref, pk_ref, qg_ref, kg_ref,
                   freq_ref, q_ref, k_ref, v_ref):
    nq = SWA_QH * SWA_HD
    nk = 2 * SWA_KVH * SWA_HD
    xn = _rms(h_ref[...], gain_ref[...]).astype(BF16)
    proj = _dot(xn, w_ref[...]) + b_ref[...]
    q = proj[:, 0:nq]
    k = proj[:, nq:nq + nk]
    v = proj[:, nq + nk:nq + 2 * nk]

    def headnorm(t, p_ref, gain):
        s1, s2 = _split2(t * t)
        ms = _dot(s1, p_ref[...]) + _dot(s2, p_ref[...])
        return t * lax.rsqrt(ms + RMS_EPS) * gain

    q = headnorm(q, pq_ref, qg_ref[...])
    k = headnorm(k, pk_ref, kg_ref[...])

    ang = pos_ref[...].astype(F32) * freq_ref[...]
    cs = jnp.cos(ang)
    sn = jnp.sin(ang)
    hl = lax.broadcasted_iota(jnp.int32, ang.shape, 1) & (SWA_HD - 1)
    half = ROPE_DIM // 2
    s_lo = jnp.where(hl < half, -sn, 0.0)
    s_hi = jnp.where((hl >= half) & (hl < ROPE_DIM), sn, 0.0)

    def rope(t):
        outs = []
        for c in range(t.shape[1] // LANES):
            x = t[:, c * LANES:(c + 1) * LANES]
            x_up = pltpu.roll(x, LANES - half, axis=1)
            x_dn = pltpu.roll(x, half, axis=1)
            outs.append(x * cs + x_up * s_lo + x_dn * s_hi)
        return jnp.concatenate(outs, axis=1)

    q_ref[...] = (rope(q) * (SWA_HD ** -0.5)).astype(BF16)
    k_ref[...] = rope(k).astype(BF16)
    v_ref[...] = v.astype(BF16)


def _swa_in(h2, pos, gain, w, b, pq, pk, qg, kg, freq):
    T = h2.shape[0]
    nq = SWA_QH * SWA_HD
    nk = 2 * SWA_KVH * SWA_HD
    tm = TOKEN_TILE
    row = lambda n: pl.BlockSpec((tm, n), lambda i: (i, 0))
    full = lambda a: pl.BlockSpec(a.shape, lambda i: (0,) * a.ndim)
    return pl.pallas_call(
        _swa_in_kernel,
        grid=(T // tm,),
        in_specs=[row(D_MODEL), row(1), full(gain), full(w), full(b), full(pq), full(pk),
                  full(qg), full(kg), full(freq)],
        out_specs=[row(nq), row(nk), row(nk)],
        out_shape=[jax.ShapeDtypeStruct((T, nq), BF16), jax.ShapeDtypeStruct((T, nk), BF16),
                   jax.ShapeDtypeStruct((T, nk), BF16)],
        compiler_params=_cparams(),
        name="swa_in",
    )(h2, pos, gain, w, b, pq, pk, qg, kg, freq)


def _swa_attn_kernel(sink_ref, q_ref, kc_ref, kp_ref, vc_ref, vp_ref, o_ref):
    n = pl.program_id(1)
    blk = SWA_BLK
    kk = jnp.concatenate([kp_ref[...], kc_ref[...]], axis=0)
    vv = jnp.concatenate([vp_ref[...], vc_ref[...]], axis=0)
    qi = lax.broadcasted_iota(jnp.int32, (blk, 2 * blk), 0)
    ki = lax.broadcasted_iota(jnp.int32, (blk, 2 * blk), 1)
    rel = qi + blk - ki
    first_key = jnp.where(n > 0, 0, blk)
    mask = (rel >= 0) & (rel < blk) & (ki >= first_key)
    lane = lax.broadcasted_iota(jnp.int32, (blk, LANES), 1)
    first = lane < SWA_HD
    for j in range(SWA_KVH):
        k2 = kk[:, j * LANES:(j + 1) * LANES]
        v2 = vv[:, j * LANES:(j + 1) * LANES]
        for gp in range(SWA_GROUP // 2):
            c0 = j * (SWA_GROUP * SWA_HD) + gp * LANES
            qp = q_ref[:, c0:c0 + LANES]
            outs = []
            for hf in range(2):
                qm = jnp.where(first if hf == 0 else ~first, qp, jnp.zeros_like(qp))
                s = jnp.where(mask, _dot_nt(qm, k2), -jnp.inf)
                sink = sink_ref[j * SWA_GROUP + 2 * gp + hf]
                m = jnp.maximum(jnp.max(s, axis=-1, keepdims=True), sink)
                p = jnp.exp(s - m)
                denom = jnp.sum(p, axis=-1, keepdims=True) + jnp.exp(sink - m)
                outs.append(_dot((p / denom).astype(BF16), v2))
            o_ref[:, c0:c0 + LANES] = jnp.where(first, outs[0], outs[1]).astype(BF16)


def _swa_attn(sinks, q, k2, v2):
    B, S, nq = q.shape
    nk = k2.shape[2]
    blk = SWA_BLK
    cur = lambda n_: pl.BlockSpec((None, blk, n_), lambda b, n: (b, n, 0))
    prev = lambda n_: pl.BlockSpec((None, blk, n_), lambda b, n: (b, jnp.maximum(n - 1, 0), 0))
    return pl.pallas_call(
        _swa_attn_kernel,
        grid=(B, S // blk),
        in_specs=[pl.BlockSpec(memory_space=pltpu.SMEM), cur(nq), cur(nk), prev(nk), cur(nk), prev(nk)],
        out_specs=cur(nq),
        out_shape=jax.ShapeDtypeStruct((B, S, nq), BF16),
        compiler_params=_cparams(2),
        name="swa_attn",
    )(sinks, q, k2, k2, v2, v2)


def _top16_rows(a, nrows):
    rid = lax.broadcasted_iota(jnp.int32, a.shape, 0)
    vals, ids = [], []
    for _ in range(PEER_TOPK):
        m = jnp.max(a, axis=0, keepdims=True)
        am = jnp.min(jnp.where(a == m, rid, nrows), axis=0, keepdims=True)
        vals.append(m)
        ids.append(am)
        a = jnp.where(rid == am, -jnp.inf, a)
    return jnp.concatenate(vals, axis=0), jnp.concatenate(ids, axis=0)


def _staircase_blocks():
    K = PEER_TOPK
    blocks = []
    for i in range(4):
        nj = K // (i + 1)
        for j0 in range(0, nj, 8):
            blocks.append(("j", i, j0, [j0 + r < nj for r in range(8)]))
    for j in range(3):
        for i0 in (0, 8):
            keep = [(i0 + r >= 4) and ((i0 + r + 1) * (j + 1) <= K) for r in range(8)]
            if any(keep):
                blocks.append(("i", j, i0, keep))
    return blocks


_STAIR = _staircase_blocks()


def _peer_select_kernel(h_ref, gain_ref, wq_ref, kbd_ref, xn_ref, idx_ref, gate_ref):
    K = PEER_TOPK
    tm = h_ref.shape[0]
    xn = _rms(h_ref[...], gain_ref[...])
    xn_ref[...] = xn
    qv = _dot(xn.astype(BF16), wq_ref[...]).astype(BF16)
    st = _dot_nt(kbd_ref[...], qv)
    sub = lax.broadcasted_iota(jnp.int32, (8, tm), 0)
    experts = []
    for h in range(PEER_HEADS):
        r0 = 2 * h * PEER_KEYS
        v1, i1 = _top16_rows(st[r0:r0 + PEER_KEYS], PEER_KEYS)
        v2, i2 = _top16_rows(st[r0 + PEER_KEYS:r0 + 2 * PEER_KEYS], PEER_KEYS)
        cv, cc = [], []
        for kind, fixed, start, keep in _STAIR:
            keepm = functools.reduce(jnp.logical_or, [sub == r for r in range(8) if keep[r]])
            if kind == "j":
                val = v1[fixed:fixed + 1] + v2[start:start + 8]
                code = fixed * K + start + sub
            else:
                val = v1[start:start + 8] + v2[fixed:fixed + 1]
                code = (start + sub) * K + fixed
            cv.append(jnp.where(keepm, val, -jnp.inf))
            cc.append(code)
        cand = jnp.concatenate(cv, axis=0)
        code = jnp.concatenate(cc, axis=0)
        scs, cis = [], []
        for _ in range(K):
            m = jnp.max(cand, axis=0, keepdims=True)
            cm = jnp.min(jnp.where(cand == m, code, K * K), axis=0, keepdims=True)
            scs.append(m)
            cis.append(cm)
            cand = jnp.where(code == cm, -jnp.inf, cand)
        sc = jnp.concatenate(scs, axis=0)
        ci = jnp.concatenate(cis, axis=0)
        chi = ci >> 4
        clo = ci & (K - 1)
        e1 = jnp.zeros_like(ci)
        e2 = jnp.zeros_like(ci)
        for i in range(K):
            e1 = jnp.where(chi == i, i1[i:i + 1], e1)
            e2 = jnp.where(clo == i, i2[i:i + 1], e2)
        experts.append(e1 * PEER_KEYS + e2)
        ex = jnp.exp(sc - sc[0:1])
        gate_ref[h * K:(h + 1) * K, :] = ex / jnp.sum(ex, axis=0, keepdims=True)
    idx_ref[...] = jnp.concatenate(experts, axis=0).T


def _peer_select(h2, gain, wq, kbd):
    T = h2.shape[0]
    tm = TOKEN_TILE
    full = lambda a: pl.BlockSpec(a.shape, lambda i: (0,) * a.ndim)
    return pl.pallas_call(
        _peer_select_kernel,
        grid=(T // tm,),
        in_specs=[pl.BlockSpec((tm, D_MODEL), lambda i: (i, 0)), full(gain), full(wq), full(kbd)],
        out_specs=[pl.BlockSpec((tm, D_MODEL), lambda i: (i, 0)),
                   pl.BlockSpec((tm, PEER_SLOTS), lambda i: (i, 0)),
                   pl.BlockSpec((PEER_SLOTS, tm), lambda i: (0, i))],
        out_shape=[jax.ShapeDtypeStruct((T, D_MODEL), F32),
                   jax.ShapeDtypeStruct((T, PEER_SLOTS), jnp.int32),
                   jax.ShapeDtypeStruct((PEER_SLOTS, T), F32)],
        compiler_params=_cparams(),
        name="peer_select",
    )(h2, gain, wq, kbd)


def _pack_table(w):
    wb = w.astype(BF16)
    half = D_MODEL // 2
    lo = lax.bitcast_convert_type(wb[:, :half], jnp.uint16).astype(jnp.uint32)
    hi = lax.bitcast_convert_type(wb[:, half:], jnp.uint16).astype(jnp.uint32)
    return (lo | (hi << 16)).reshape(w.shape[0], 4, LANES)


def _unpack(w):
    lo = lax.bitcast_convert_type(w << 16, F32)
    hi = lax.bitcast_convert_type(w & jnp.uint32(0xFFFF0000), F32)
    return lo, hi


def _stream_index_groups(idx_ref, ibufs, sems, process):
    ng = PEER_TOKENS // IDX_GROUP

    def copy(g, slot):
        rows = pl.ds(pl.multiple_of(g * IDX_GROUP, IDX_GROUP), IDX_GROUP)
        return pltpu.make_async_copy(idx_ref.at[rows], ibufs[slot], sems.at[slot])

    copy(0, 0).start()

    def pair(p, carry):
        g = 2 * p
        copy(g, 0).wait()
        copy(g + 1, 1).start()
        process(g, ibufs[0])
        copy(g + 1, 1).wait()
        nxt = jnp.minimum(g + 2, ng - 1)
        copy(nxt, 0).start()
        process(g + 1, ibufs[1])
        return carry

    lax.fori_loop(0, ng // 2, pair, 0)
    copy(ng - 1, 0).wait()


def _index_scratch():
    return [pltpu.SMEM((IDX_GROUP, PEER_SLOTS), jnp.int32), pltpu.SMEM((IDX_GROUP, PEER_SLOTS), jnp.int32),
            pltpu.SemaphoreType.DMA((2,))]


def _peer_u_kernel(idx_ref, xlo_ref, xhi_ref, gate_ref, tab_ref, a_ref,
                   s0_ref, s1_ref, ib0_ref, ib1_ref, sems):
    tb = PEER_TOKENS
    ss = FOLD_STRIDE
    lane = lax.broadcasted_iota(jnp.int32, (8, tb), 1)
    folds = (s0_ref, s1_ref)

    def gather(t, j, ibuf, s_ref):
        xlo = xlo_ref[t]
        xhi = xhi_ref[t]
        for k in range(PEER_SLOTS):
            lo, hi = _unpack(tab_ref[ibuf[j, k]])
            s_ref[pl.ds(k, 4, stride=ss), :] = lo * xlo + hi * xhi

    def fold(t, s_ref):
        sel = lane == t
        for j in range(PEER_SLOTS // 8):
            r = s_ref[pl.ds(8 * j, 8), :]
            for s in range(1, 4):
                r = r + s_ref[pl.ds(s * ss + 8 * j, 8), :]
            tot = jnp.sum(r, axis=1, keepdims=True)
            a_ref[pl.ds(8 * j, 8), :] = jnp.where(sel, tot, a_ref[pl.ds(8 * j, 8), :])

    def process(g, ibuf):
        for j in range(IDX_GROUP):
            t = g * IDX_GROUP + j
            gather(t, j, ibuf, folds[j % 2])
            fold(t - 1, folds[(j + 1) % 2])

    a_ref[...] = jnp.zeros_like(a_ref)
    s1_ref[...] = jnp.zeros_like(s1_ref)
    _stream_index_groups(idx_ref, (ib0_ref, ib1_ref), sems, process)
    fold(tb - 1, folds[(tb - 1) % 2])
    z = a_ref[...]
    a_ref[...] = 0.5 * z * (1.0 + lax.erf(z * (2.0 ** -0.5))) * gate_ref[...]


def _peer_u(idx, xlo, xhi, gate_t, tab):
    T = idx.shape[0]
    tb = PEER_TOKENS
    ne = tab.shape[0]
    return pl.pallas_call(
        _peer_u_kernel,
        grid=(T // tb,),
        in_specs=[
            pl.BlockSpec((tb, PEER_SLOTS), lambda i: (i, 0)),
            pl.BlockSpec((tb, 4, LANES), lambda i: (i, 0, 0)),
            pl.BlockSpec((tb, 4, LANES), lambda i: (i, 0, 0)),
            pl.BlockSpec((PEER_SLOTS, tb), lambda i: (0, i)),
            pl.BlockSpec((ne, 4, LANES), lambda i: (0, 0, 0), pipeline_mode=pl.Buffered(1)),
        ],
        out_specs=pl.BlockSpec((PEER_SLOTS, tb), lambda i: (0, i)),
        out_shape=jax.ShapeDtypeStruct((PEER_SLOTS, T), F32),
        scratch_shapes=[pltpu.VMEM((4 * FOLD_STRIDE, LANES), F32),
                        pltpu.VMEM((4 * FOLD_STRIDE, LANES), F32)] + _index_scratch(),
        compiler_params=_cparams(),
        name="peer_u",
    )(idx, xlo, xhi, gate_t, tab)


def _peer_v_kernel(idx_ref, a_ref, rlo_ref, rhi_ref, tab_ref, olo_ref, ohi_ref,
                   b0_ref, b1_ref, ib0_ref, ib1_ref, sems):
    tb = PEER_TOKENS
    lane = lax.broadcasted_iota(jnp.int32, (PEER_SLOTS, tb), 1)
    bcast = (b0_ref, b1_ref)

    def prep(t, b_ref):
        col = jnp.sum(jnp.where(lane == t, a_ref[...], 0.0), axis=1, keepdims=True)
        b_ref[...] = jnp.broadcast_to(col, (PEER_SLOTS, LANES))

    def accum(t, j, ibuf, b_ref):
        acc_lo = [rlo_ref[t], jnp.zeros((4, LANES), F32)]
        acc_hi = [rhi_ref[t], jnp.zeros((4, LANES), F32)]
        for k in range(PEER_SLOTS):
            lo, hi = _unpack(tab_ref[ibuf[j, k]])
            ab = jnp.broadcast_to(b_ref[pl.ds(k, 1), :], (4, LANES))
            acc_lo[k % 2] = acc_lo[k % 2] + ab * lo
            acc_hi[k % 2] = acc_hi[k % 2] + ab * hi
        olo_ref[t] = acc_lo[0] + acc_lo[1]
        ohi_ref[t] = acc_hi[0] + acc_hi[1]

    def process(g, ibuf):
        for j in range(IDX_GROUP):
            t = g * IDX_GROUP + j
            prep(jnp.minimum(t + 1, tb - 1), bcast[(j + 1) % 2])
            accum(t, j, ibuf, bcast[j % 2])

    prep(0, b0_ref)
    _stream_index_groups(idx_ref, (ib0_ref, ib1_ref), sems, process)


def _peer_v(idx, a_t, rlo, rhi, tab):
    T = idx.shape[0]
    tb = PEER_TOKENS
    ne = tab.shape[0]
    tok = pl.BlockSpec((tb, 4, LANES), lambda i: (i, 0, 0))
    return pl.pallas_call(
        _peer_v_kernel,
        grid=(T // tb,),
        in_specs=[
            pl.BlockSpec((tb, PEER_SLOTS), lambda i: (i, 0)),
            pl.BlockSpec((PEER_SLOTS, tb), lambda i: (0, i)),
            tok, tok,
            pl.BlockSpec((ne, 4, LANES), lambda i: (0, 0, 0), pipeline_mode=pl.Buffered(1)),
        ],
        out_specs=[tok, tok],
        out_shape=[jax.ShapeDtypeStruct((T, 4, LANES), F32), jax.ShapeDtypeStruct((T, 4, LANES), F32)],
        scratch_shapes=[pltpu.VMEM((PEER_SLOTS, LANES), F32), pltpu.VMEM((PEER_SLOTS, LANES), F32)]
        + _index_scratch(),
        compiler_params=_cparams(),
        name="peer_v",
    )(idx, a_t, rlo, rhi, tab)


def _halves(a2):
    a4 = a2.reshape(a2.shape[0], 2, 4, LANES)
    return a4[:, 0], a4[:, 1]


def _peer_layer(h2, gain, wq, keys, u, v):
    T = h2.shape[0]
    eye = jnp.eye(PEER_HEADS * 2, dtype=F32)
    kbd = jnp.einsum("pkd,pq->pkqd", keys.reshape(PEER_HEADS * 2, PEER_KEYS, PEER_HALF), eye)
    kbd = kbd.reshape(PEER_HEADS * 2 * PEER_KEYS, D_MODEL).astype(BF16)
    xn, idx, gate_t = _peer_select(h2, gain.reshape(1, D_MODEL), wq.astype(BF16), kbd)
    xlo, xhi = _halves(xn)
    a_t = _peer_u(idx, xlo, xhi, gate_t, _pack_table(u))
    rlo, rhi = _halves(h2)
    olo, ohi = _peer_v(idx, a_t, rlo, rhi, _pack_table(v))
    return jnp.concatenate([olo.reshape(T, D_MODEL // 2), ohi.reshape(T, D_MODEL // 2)], axis=1)


def _gla_layer(x2, B, S, gain, w_in, w_alpha, b_alpha, norm_gain, w_out):
    hk = GLA_HEADS * GLA_DK
    hv = GLA_HEADS * GLA_DV
    w_main = w_in[:, :2 * hk + 2 * hv].astype(BF16)
    w_lr = jnp.pad(w_in[:, 2 * hk + 2 * hv:], ((0, 0), (0, LANES - GLA_RANK))).astype(BF16)
    wa = jnp.pad(w_alpha, ((0, LANES - GLA_RANK), (0, 0))).astype(BF16)
    q, k, v, r, g = _gla_in(x2, gain.reshape(1, D_MODEL), w_main, w_lr, wa, b_alpha.reshape(1, hk))
    sh = lambda a: a.reshape(B, S, a.shape[1])
    og = _gla_scan(sh(q), sh(k), sh(g), sh(v), sh(r), norm_gain.reshape(1, hv))
    return _proj_res(og.reshape(B * S, hv), w_out.astype(BF16), None, x2, "gla_out")


def _swa_layer(h2, pos, B, S, gain, w_in, b_in, q_gain, k_gain, sinks, w_out, b_out):
    nq = SWA_QH * SWA_HD
    hd = SWA_HD
    wq_, wk_, wv_ = w_in[:, :nq], w_in[:, nq:nq + SWA_KVH * hd], w_in[:, nq + SWA_KVH * hd:]
    bq_, bk_, bv_ = b_in[:nq], b_in[nq:nq + SWA_KVH * hd], b_in[nq + SWA_KVH * hd:]

    def dup(a):
        parts = []
        for j in range(SWA_KVH):
            parts += [a[..., j * hd:(j + 1) * hd]] * 2
        return jnp.concatenate(parts, axis=-1)

    w = jnp.concatenate([wq_, dup(wk_), dup(wv_)], axis=1).astype(BF16)
    b = jnp.concatenate([bq_, dup(bk_), dup(bv_)]).reshape(1, -1)
    nk = 2 * SWA_KVH * hd
    pmat = lambda n: (jnp.kron(jnp.eye(n // hd, dtype=F32), jnp.ones((hd, hd), F32)) / hd).astype(BF16)
    inv_freq = ROPE_THETA ** (-jnp.arange(0, ROPE_DIM, 2, dtype=F32) / ROPE_DIM)
    hl = np.arange(LANES) % hd
    freq = jnp.where(hl < ROPE_DIM, inv_freq[hl % (ROPE_DIM // 2)], 0.0).reshape(1, LANES)
    q, k2, v2 = _swa_in(h2, pos.reshape(B * S, 1), gain.reshape(1, D_MODEL), w, b, pmat(nq), pmat(nk),
                        jnp.tile(q_gain, nq // hd).reshape(1, nq), jnp.tile(k_gain, nk // hd).reshape(1, nk), freq)
    o = _swa_attn(sinks, q.reshape(B, S, nq), k2.reshape(B, S, nk), v2.reshape(B, S, nk))
    return _proj_res(o.reshape(B * S, nq), w_out.astype(BF16), b_out.reshape(1, D_MODEL), h2, "swa_out")


def kernel(x, positions, ln_mix, ln_ffn, gla_w_in, gla_w_alpha, gla_b_alpha, gla_norm, gla_w_out, swa_w_in, swa_b_in, swa_q_norm, swa_k_norm, swa_sinks, swa_w_out, swa_b_out, peer_w_q, peer_keys, peer_u, peer_v):
    B, S, D = x.shape
    h = x.reshape(B * S, D)
    h = _gla_layer(h, B, S, ln_mix[0], gla_w_in[0], gla_w_alpha[0], gla_b_alpha[0], gla_norm[0], gla_w_out[0])
    h = _peer_layer(h, ln_ffn[0], peer_w_q[0], peer_keys[0], peer_u[0], peer_v[0])
    h = _swa_layer(h, positions, B, S, ln_mix[1], swa_w_in[0], swa_b_in[0], swa_q_norm[0], swa_k_norm[0],
                   swa_sinks[0], swa_w_out[0], swa_b_out[0])
    h = _peer_layer(h, ln_ffn[1], peer_w_q[1], peer_keys[1], peer_u[1], peer_v[1])
    return h.reshape(B, S, D)
```

```python
import functools

import jax
import jax.numpy as jnp
import numpy as np
from jax import lax
from jax.experimental import pallas as pl
from jax.experimental.pallas import tpu as pltpu

F32 = jnp.float32
BF16 = jnp.bfloat16

D_MODEL = 1024
RMS_EPS = 1e-6

GLA_HEADS = 4
GLA_DK = 128
GLA_DV = 256
GLA_RANK = 16
GLA_TAU = 16.0
GLA_CHUNK = 64

SWA_HD = 64
SWA_QH = 16
SWA_KVH = 2
SWA_GROUP = 8
SWA_BLK = 128
ROPE_THETA = 500000.0
ROPE_DIM = 16

PEER_HEADS = 8
PEER_KEYS = 128
PEER_EXPERTS = PEER_KEYS * PEER_KEYS
PEER_HALF = 64
PEER_TOPK = 16
PEER_SLOTS = PEER_HEADS * PEER_TOPK

LANES = 128
TOKEN_TILE = 256
PEER_TOKENS = 128
IDX_GROUP = 8
FOLD_STRIDE = 136
VMEM_LIMIT = 48 * 1024 * 1024


def _cparams(n_axes=1):
    return pltpu.CompilerParams(
        dimension_semantics=("arbitrary",) * n_axes, vmem_limit_bytes=VMEM_LIMIT)


def _rms(x, gain):
    ms = jnp.mean(x * x, axis=-1, keepdims=True)
    return x * lax.rsqrt(ms + RMS_EPS) * gain


def _dot(a, b):
    return jnp.dot(a, b, preferred_element_type=F32)


def _dot_nt(a, b):
    return lax.dot_general(a, b, (((1,), (1,)), ((), ())), preferred_element_type=F32)


def _dot_tn(a, b):
    return lax.dot_general(a, b, (((0,), (0,)), ((), ())), preferred_element_type=F32)


def _split2(x):
    hi = x.astype(BF16)
    lo = (x - hi.astype(F32)).astype(BF16)
    return hi, lo


def _split3(x):
    a = x.astype(BF16)
    r = x - a.astype(F32)
    b = r.astype(BF16)
    c = (r - b.astype(F32)).astype(BF16)
    return a, b, c


def _gla_in_kernel(x_ref, gain_ref, w_ref, wlr_ref, wa_ref, ba_ref,
                   q_ref, k_ref, v_ref, r_ref, g_ref):
    hk = GLA_HEADS * GLA_DK
    hv = GLA_HEADS * GLA_DV
    xn = _rms(x_ref[...], gain_ref[...]).astype(BF16)
    q_ref[...] = _dot(xn, w_ref[:, 0:hk])
    k_ref[...] = _dot(xn, w_ref[:, hk:2 * hk])
    v_ref[...] = _dot(xn, w_ref[:, 2 * hk:2 * hk + hv]).astype(BF16)
    r_ref[...] = _dot(xn, w_ref[:, 2 * hk + hv:2 * hk + 2 * hv])
    lr = _dot(xn, wlr_ref[...])
    z = _dot(lr.astype(BF16), wa_ref[...]) + ba_ref[...]
    g_ref[...] = (jnp.minimum(z, 0.0) - jnp.log(1.0 + jnp.exp(-jnp.abs(z)))) / GLA_TAU


def _gla_in(x2, gain, w_main, w_lr, w_alpha, b_alpha):
    T = x2.shape[0]
    hk = GLA_HEADS * GLA_DK
    hv = GLA_HEADS * GLA_DV
    tm = TOKEN_TILE
    row = lambda n: pl.BlockSpec((tm, n), lambda i: (i, 0))
    full = lambda a: pl.BlockSpec(a.shape, lambda i: (0,) * a.ndim)
    return pl.pallas_call(
        _gla_in_kernel,
        grid=(T // tm,),
        in_specs=[row(D_MODEL), full(gain), full(w_main), full(w_lr), full(w_alpha), full(b_alpha)],
        out_specs=[row(hk), row(hk), row(hv), row(hv), row(hk)],
        out_shape=[jax.ShapeDtypeStruct((T, hk), F32), jax.ShapeDtypeStruct((T, hk), F32),
                   jax.ShapeDtypeStruct((T, hv), BF16), jax.ShapeDtypeStruct((T, hv), F32),
                   jax.ShapeDtypeStruct((T, hk), F32)],
        compiler_params=_cparams(),
        name="gla_in",
    )(x2, gain, w_main, w_lr, w_alpha, b_alpha)


def _gla_scan_kernel(q_ref, k_ref, g_ref, v_ref, r_ref, gain_ref, o_ref, st_ref):
    C = GLA_CHUNK
    nb = q_ref.shape[0]

    @pl.when(pl.program_id(0) == 0)
    def _():
        st_ref[...] = jnp.zeros_like(st_ref)

    ri = lax.broadcasted_iota(jnp.int32, (C, C), 0)
    ci = lax.broadcasted_iota(jnp.int32, (C, C), 1)
    tri = ri >= ci
    tri_b = jnp.where(tri, 1.0, 0.0).astype(BF16)
    for b in range(nb):
        g1, g2, g3 = _split3(g_ref[b])
        bc = _dot(tri_b, g1) + _dot(tri_b, g2) + _dot(tri_b, g3)
        eb = jnp.exp(bc)
        enb = jnp.exp(-bc)
        ebl = jnp.exp(bc[C - 1:C, :])
        qd = q_ref[b] * (GLA_DK ** -0.5) * eb
        ki = k_ref[b] * enb
        kf = ki * ebl
        for h in range(GLA_HEADS):
            ks = slice(h * GLA_DK, (h + 1) * GLA_DK)
            vs = slice(h * GLA_DV, (h + 1) * GLA_DV)
            qd_h = qd[:, ks].astype(BF16)
            ki_h = ki[:, ks].astype(BF16)
            kf_h = kf[:, ks].astype(BF16)
            v_h = v_ref[b, :, vs]
            att = jnp.where(tri, _dot_nt(qd_h, ki_h), 0.0).astype(BF16)
            st = st_ref[b * GLA_HEADS + h]
            o = _dot(att, v_h) + _dot_nt(qd_h, st.astype(BF16))
            st_ref[b * GLA_HEADS + h] = st * ebl[:, ks] + _dot_tn(v_h, kf_h)
            ms = jnp.mean(o * o, axis=-1, keepdims=True)
            on = o * lax.rsqrt(ms + RMS_EPS) * gain_ref[:, vs]
            rr = r_ref[b, :, vs]
            o_ref[b, :, vs] = (on * (rr / (1.0 + jnp.exp(-rr)))).astype(BF16)


def _gla_scan(q, k, g, v, r, gain):
    B, S, hk = q.shape
    hv = v.shape[2]
    C = GLA_CHUNK
    blk = lambda n: pl.BlockSpec((B, C, n), lambda c: (0, c, 0))
    return pl.pallas_call(
        _gla_scan_kernel,
        grid=(S // C,),
        in_specs=[blk(hk), blk(hk), blk(hk), blk(hv), blk(hv),
                  pl.BlockSpec(gain.shape, lambda c: (0, 0))],
        out_specs=blk(hv),
        out_shape=jax.ShapeDtypeStruct((B, S, hv), BF16),
        scratch_shapes=[pltpu.VMEM((B * GLA_HEADS, GLA_DV, GLA_DK), F32)],
        compiler_params=_cparams(),
        name="gla_scan",
    )(q, k, g, v, r, gain)


def _proj_res_kernel(a_ref, w_ref, res_ref, o_ref):
    o_ref[...] = res_ref[...] + _dot(a_ref[...], w_ref[...])


def _proj_bias_res_kernel(a_ref, w_ref, b_ref, res_ref, o_ref):
    o_ref[...] = res_ref[...] + (_dot(a_ref[...], w_ref[...]) + b_ref[...])


def _proj_res(a, w, bias, res, name):
    T, K = a.shape
    N = w.shape[1]
    tm = TOKEN_TILE
    row = lambda n: pl.BlockSpec((tm, n), lambda i: (i, 0))
    wspec = pl.BlockSpec((K, N), lambda i: (0, 0))
    if bias is None:
        body, specs, args = _proj_res_kernel, [row(K), wspec, row(N)], (a, w, res)
    else:
        body = _proj_bias_res_kernel
        specs = [row(K), wspec, pl.BlockSpec((1, N), lambda i: (0, 0)), row(N)]
        args = (a, w, bias, res)
    return pl.pallas_call(
        body,
        grid=(T // tm,),
        in_specs=specs,
        out_specs=row(N),
        out_shape=jax.ShapeDtypeStruct((T, N), F32),
        compiler_params=_cparams(),
        name=name,
    )(*args)


def _swa_in_kernel(h_ref, pos_ref, gain_ref, w_ref, b_ref, pq_ref, pk_ref, qg_ref, kg_ref,
                   freq_ref, q_ref, k_ref, v_ref):
    nq = SWA_QH * SWA_HD
    nk = 2 * SWA_KVH * SWA_HD
    xn = _rms(h_ref[...], gain_ref[...]).astype(BF16)
    proj = _dot(xn, w_ref[...]) + b_ref[...]
    q = proj[:, 0:nq]
    k = proj[:, nq:nq + nk]
    v = proj[:, nq + nk:nq + 2 * nk]

    def headnorm(t, p_ref, gain):
        s1, s2 = _split2(t * t)
        ms = _dot(s1, p_ref[...]) + _dot(s2, p_ref[...])
        return t * lax.rsqrt(ms + RMS_EPS) * gain

    q = headnorm(q, pq_ref, qg_ref[...])
    k = headnorm(k, pk_ref, kg_ref[...])

    ang = pos_ref[...].astype(F32) * freq_ref[...]
    cs = jnp.cos(ang)
    sn = jnp.sin(ang)
    hl = lax.broadcasted_iota(jnp.int32, ang.shape, 1) & (SWA_HD - 1)
    half = ROPE_DIM // 2
    s_lo = jnp.where(hl < half, -sn, 0.0)
    s_hi = jnp.where((hl >= half) & (hl < ROPE_DIM), sn, 0.0)

    def rope(t):
        outs = []
        for c in range(t.shape[1] // LANES):
            x = t[:, c * LANES:(c + 1) * LANES]
            x_up = pltpu.roll(x, LANES - half, axis=1)
            x_dn = pltpu.roll(x, half, axis=1)
            outs.append(x * cs + x_up * s_lo + x_dn * s_hi)
        return jnp.concatenate(outs, axis=1)

    q_ref[...] = (rope(q) * (SWA_HD ** -0.5)).astype(BF16)
    k_ref[...] = rope(k).astype(BF16)
    v_ref[...] = v.astype(BF16)


def _swa_in(h2, pos, gain, w, b, pq, pk, qg, kg, freq):
    T = h2.shape[0]
    nq = SWA_QH * SWA_HD
    nk = 2 * SWA_KVH * SWA_HD
    tm = TOKEN_TILE
    row = lambda n: pl.BlockSpec((tm, n), lambda i: (i, 0))
    full = lambda a: pl.BlockSpec(a.shape, lambda i: (0,) * a.ndim)
    return pl.pallas_call(
        _swa_in_kernel,
        grid=(T // tm,),
        in_specs=[row(D_MODEL), row(1), full(gain), full(w), full(b), full(pq), full(pk),
                  full(qg), full(kg), full(freq)],
        out_specs=[row(nq), row(nk), row(nk)],
        out_shape=[jax.ShapeDtypeStruct((T, nq), BF16), jax.ShapeDtypeStruct((T, nk), BF16),
                   jax.ShapeDtypeStruct((T, nk), BF16)],
        compiler_params=_cparams(),
        name="swa_in",
    )(h2, pos, gain, w, b, pq, pk, qg, kg, freq)


def _swa_attn_kernel(sink_ref, q_ref, kc_ref, kp_ref, vc_ref, vp_ref, o_ref):
    n = pl.program_id(1)
    blk = SWA_BLK
    kk = jnp.concatenate([kp_ref[...], kc_ref[...]], axis=0)
    vv = jnp.concatenate([vp_ref[...], vc_ref[...]], axis=0)
    qi = lax.broadcasted_iota(jnp.int32, (blk, 2 * blk), 0)
    ki = lax.broadcasted_iota(jnp.int32, (blk, 2 * blk), 1)
    rel = qi + blk - ki
    first_key = jnp.where(n > 0, 0, blk)
    mask = (rel >= 0) & (rel < blk) & (ki >= first_key)
    lane = lax.broadcasted_iota(jnp.int32, (blk, LANES), 1)
    first = lane < SWA_HD
    for j in range(SWA_KVH):
        k2 = kk[:, j * LANES:(j + 1) * LANES]
        v2 = vv[:, j * LANES:(j + 1) * LANES]
        for gp in range(SWA_GROUP // 2):
            c0 = j * (SWA_GROUP * SWA_HD) + gp * LANES
            qp = q_ref[:, c0:c0 + LANES]
            outs = []
            for hf in range(2):
                qm = jnp.where(first if hf == 0 else ~first, qp, jnp.zeros_like(qp))
                s = jnp.where(mask, _dot_nt(qm, k2), -jnp.inf)
                sink = sink_ref[j * SWA_GROUP + 2 * gp + hf]
                m = jnp.maximum(jnp.max(s, axis=-1, keepdims=True), sink)
                p = jnp.exp(s - m)
                denom = jnp.sum(p, axis=-1, keepdims=True) + jnp.exp(sink - m)
                outs.append(_dot((p / denom).astype(BF16), v2))
            o_ref[:, c0:c0 + LANES] = jnp.where(first, outs[0], outs[1]).astype(BF16)


def _swa_attn(sinks, q, k2, v2):
    B, S, nq = q.shape
    nk = k2.shape[2]
    blk = SWA_BLK
    cur = lambda n_: pl.BlockSpec((None, blk, n_), lambda b, n: (b, n, 0))
    prev = lambda n_: pl.BlockSpec((None, blk, n_), lambda b, n: (b, jnp.maximum(n - 1, 0), 0))
    return pl.pallas_call(
        _swa_attn_kernel,
        grid=(B, S // blk),
        in_specs=[pl.BlockSpec(memory_space=pltpu.SMEM), cur(nq), cur(nk), prev(nk), cur(nk), prev(nk)],
        out_specs=cur(nq),
        out_shape=jax.ShapeDtypeStruct((B, S, nq), BF16),
        compiler_params=_cparams(2),
        name="swa_attn",
    )(sinks, q, k2, k2, v2, v2)


def _top16_rows(a, nrows):
    rid = lax.broadcasted_iota(jnp.int32, a.shape, 0)
    vals, ids = [], []
    for _ in range(PEER_TOPK):
        m = jnp.max(a, axis=0, keepdims=True)
        am = jnp.min(jnp.where(a == m, rid, nrows), axis=0, keepdims=True)
        vals.append(m)
        ids.append(am)
        a = jnp.where(rid == am, -jnp.inf, a)
    return jnp.concatenate(vals, axis=0), jnp.concatenate(ids, axis=0)


def _staircase_blocks():
    K = PEER_TOPK
    blocks = []
    for i in range(4):
        nj = K // (i + 1)
        for j0 in range(0, nj, 8):
            blocks.append(("j", i, j0, [j0 + r < nj for r in range(8)]))
    for j in range(3):
        for i0 in (0, 8):
            keep = [(i0 + r >= 4) and ((i0 + r + 1) * (j + 1) <= K) for r in range(8)]
            if any(keep):
                blocks.append(("i", j, i0, keep))
    return blocks


_STAIR = _staircase_blocks()


def _peer_select_kernel(h_ref, gain_ref, wq_ref, kbd_ref, xn_ref, idx_ref, gate_ref):
    K = PEER_TOPK
    tm = h_ref.shape[0]
    xn = _rms(h_ref[...], gain_ref[...])
    xn_ref[...] = xn
    qv = _dot(xn.astype(BF16), wq_ref[...]).astype(BF16)
    st = _dot_nt(kbd_ref[...], qv)
    sub = lax.broadcasted_iota(jnp.int32, (8, tm), 0)
    experts = []
    for h in range(PEER_HEADS):
        r0 = 2 * h * PEER_KEYS
        v1, i1 = _top16_rows(st[r0:r0 + PEER_KEYS], PEER_KEYS)
        v2, i2 = _top16_rows(st[r0 + PEER_KEYS:r0 + 2 * PEER_KEYS], PEER_KEYS)
        cv, cc = [], []
        for kind, fixed, start, keep in _STAIR:
            keepm = functools.reduce(jnp.logical_or, [sub == r for r in range(8) if keep[r]])
            if kind == "j":
                val = v1[fixed:fixed + 1] + v2[start:start + 8]
                code = fixed * K + start + sub
            else:
                val = v1[start:start + 8] + v2[fixed:fixed + 1]
                code = (start + sub) * K + fixed
            cv.append(jnp.where(keepm, val, -jnp.inf))
            cc.append(code)
        cand = jnp.concatenate(cv, axis=0)
        code = jnp.concatenate(cc, axis=0)
        scs, cis = [], []
        for _ in range(K):
            m = jnp.max(cand, axis=0, keepdims=True)
            cm = jnp.min(jnp.where(cand == m, code, K * K), axis=0, keepdims=True)
            scs.append(m)
            cis.append(cm)
            cand = jnp.where(code == cm, -jnp.inf, cand)
        sc = jnp.concatenate(scs, axis=0)
        ci = jnp.concatenate(cis, axis=0)
        chi = ci >> 4
        clo = ci & (K - 1)
        e1 = jnp.zeros_like(ci)
        e2 = jnp.zeros_like(ci)
        for i in range(K):
            e1 = jnp.where(chi == i, i1[i:i + 1], e1)
            e2 = jnp.where(clo == i, i2[i:i + 1], e2)
        experts.append(e1 * PEER_KEYS + e2)
        ex = jnp.exp(sc - sc[0:1])
        gate_ref[h * K:(h + 1) * K, :] = ex / jnp.sum(ex, axis=0, keepdims=True)
    idx_ref[...] = jnp.concatenate(experts, axis=0).T


def _peer_select(h2, gain, wq, kbd):
    T = h2.shape[0]
    tm = TOKEN_TILE
    full = lambda a: pl.BlockSpec(a.shape, lambda i: (0,) * a.ndim)
    return pl.pallas_call(
        _peer_select_kernel,
        grid=(T // tm,),
        in_specs=[pl.BlockSpec((tm, D_MODEL), lambda i: (i, 0)), full(gain), full(wq), full(kbd)],
        out_specs=[pl.BlockSpec((tm, D_MODEL), lambda i: (i, 0)),
                   pl.BlockSpec((tm, PEER_SLOTS), lambda i: (i, 0)),
                   pl.BlockSpec((PEER_SLOTS, tm), lambda i: (0, i))],
        out_shape=[jax.ShapeDtypeStruct((T, D_MODEL), F32),
                   jax.ShapeDtypeStruct((T, PEER_SLOTS), jnp.int32),
                   jax.ShapeDtypeStruct((PEER_SLOTS, T), F32)],
        compiler_params=_cparams(),
        name="peer_select",
    )(h2, gain, wq, kbd)


def _pack_table(w):
    wb = w.astype(BF16)
    half = D_MODEL // 2
    lo = lax.bitcast_convert_type(wb[:, :half], jnp.uint16).astype(jnp.uint32)
    hi = lax.bitcast_convert_type(wb[:, half:], jnp.uint16).astype(jnp.uint32)
    return (lo | (hi << 16)).reshape(w.shape[0], 4, LANES)


def _unpack(w):
    lo = lax.bitcast_convert_type(w << 16, F32)
    hi = lax.bitcast_convert_type(w & jnp.uint32(0xFFFF0000), F32)
    return lo, hi


def _stream_index_groups(idx_ref, ibufs, sems, process):
    ng = PEER_TOKENS // IDX_GROUP

    def copy(g, slot):
        rows = pl.ds(pl.multiple_of(g * IDX_GROUP, IDX_GROUP), IDX_GROUP)
        return pltpu.make_async_copy(idx_ref.at[rows], ibufs[slot], sems.at[slot])

    copy(0, 0).start()

    def pair(p, carry):
        g = 2 * p
        copy(g + 1, 1).start()
        copy(g, 0).wait()
        process(g, ibufs[0])
        nxt = jnp.minimum(g + 2, ng - 1)
        copy(nxt, 0).start()
        copy(g + 1, 1).wait()
        process(g + 1, ibufs[1])
        return carry

    lax.fori_loop(0, ng // 2, pair, 0)
    copy(ng - 1, 0).wait()


def _index_scratch():
    return [pltpu.SMEM((IDX_GROUP, PEER_SLOTS), jnp.int32), pltpu.SMEM((IDX_GROUP, PEER_SLOTS), jnp.int32),
            pltpu.SemaphoreType.DMA((2,))]


def _peer_u_kernel(idx_ref, xlo_ref, xhi_ref, gate_ref, tab_ref, a_ref,
                   s0_ref, s1_ref, ib0_ref, ib1_ref, sems):
    tb = PEER_TOKENS
    ss = FOLD_STRIDE
    lane = lax.broadcasted_iota(jnp.int32, (8, tb), 1)
    folds = (s0_ref, s1_ref)

    def gather(t, j, ibuf, s_ref):
        xlo = xlo_ref[t]
        xhi = xhi_ref[t]
        for k in range(PEER_SLOTS):
            lo, hi = _unpack(tab_ref[ibuf[j, k]])
            s_ref[pl.ds(k, 4, stride=ss), :] = lo * xlo + hi * xhi

    def fold(t, s_ref):
        sel = lane == t
        for j in range(PEER_SLOTS // 8):
            r = s_ref[pl.ds(8 * j, 8), :]
            for s in range(1, 4):
                r = r + s_ref[pl.ds(s * ss + 8 * j, 8), :]
            tot = jnp.sum(r, axis=1, keepdims=True)
            a_ref[pl.ds(8 * j, 8), :] = jnp.where(sel, tot, a_ref[pl.ds(8 * j, 8), :])

    def process(g, ibuf):
        for j in range(IDX_GROUP):
            t = g * IDX_GROUP + j
            gather(t, j, ibuf, folds[j % 2])
            fold(t - 1, folds[(j + 1) % 2])

    a_ref[...] = jnp.zeros_like(a_ref)
    s1_ref[...] = jnp.zeros_like(s1_ref)
    _stream_index_groups(idx_ref, (ib0_ref, ib1_ref), sems, process)
    fold(tb - 1, folds[(tb - 1) % 2])
    z = a_ref[...]
    a_ref[...] = 0.5 * z * (1.0 + lax.erf(z * (2.0 ** -0.5))) * gate_ref[...]


def _peer_u(idx, xlo, xhi, gate_t, tab):
    T = idx.shape[0]
    tb = PEER_TOKENS
    ne = tab.shape[0]
    return pl.pallas_call(
        _peer_u_kernel,
        grid=(T // tb,),
        in_specs=[
            pl.BlockSpec((tb, PEER_SLOTS), lambda i: (i, 0)),
            pl.BlockSpec((tb, 4, LANES), lambda i: (i, 0, 0)),
            pl.BlockSpec((tb, 4, LANES), lambda i: (i, 0, 0)),
            pl.BlockSpec((PEER_SLOTS, tb), lambda i: (0, i)),
            pl.BlockSpec((ne, 4, LANES), lambda i: (0, 0, 0), pipeline_mode=pl.Buffered(1)),
        ],
        out_specs=pl.BlockSpec((PEER_SLOTS, tb), lambda i: (0, i)),
        out_shape=jax.ShapeDtypeStruct((PEER_SLOTS, T), F32),
        scratch_shapes=[pltpu.VMEM((4 * FOLD_STRIDE, LANES), F32),
                        pltpu.VMEM((4 * FOLD_STRIDE, LANES), F32)] + _index_scratch(),
        compiler_params=_cparams(),
        name="peer_u",
    )(idx, xlo, xhi, gate_t, tab)


def _peer_v_kernel(idx_ref, a_ref, rlo_ref, rhi_ref, tab_ref, olo_ref, ohi_ref,
                   b0_ref, b1_ref, ib0_ref, ib1_ref, sems):
    tb = PEER_TOKENS
    lane = lax.broadcasted_iota(jnp.int32, (PEER_SLOTS, tb), 1)
    bcast = (b0_ref, b1_ref)

    def prep(t, b_ref):
        col = jnp.sum(jnp.where(lane == t, a_ref[...], 0.0), axis=1, keepdims=True)
        b_ref[...] = jnp.broadcast_to(col, (PEER_SLOTS, LANES))

    def accum(t, j, ibuf, b_ref):
        acc_lo = [rlo_ref[t], jnp.zeros((4, LANES), F32)]
        acc_hi = [rhi_ref[t], jnp.zeros((4, LANES), F32)]
        for k in range(PEER_SLOTS):
            lo, hi = _unpack(tab_ref[ibuf[j, k]])
            ab = jnp.broadcast_to(b_ref[pl.ds(k, 1), :], (4, LANES))
            acc_lo[k % 2] = acc_lo[k % 2] + ab * lo
            acc_hi[k % 2] = acc_hi[k % 2] + ab * hi
        olo_ref[t] = acc_lo[0] + acc_lo[1]
        ohi_ref[t] = acc_hi[0] + acc_hi[1]

    def process(g, ibuf):
        for j in range(IDX_GROUP):
            t = g * IDX_GROUP + j
            prep(jnp.minimum(t + 1, tb - 1), bcast[(j + 1) % 2])
            accum(t, j, ibuf, bcast[j % 2])

    prep(0, b0_ref)
    _stream_index_groups(idx_ref, (ib0_ref, ib1_ref), sems, process)


def _peer_v(idx, a_t, rlo, rhi, tab):
    T = idx.shape[0]
    tb = PEER_TOKENS
    ne = tab.shape[0]
    tok = pl.BlockSpec((tb, 4, LANES), lambda i: (i, 0, 0))
    return pl.pallas_call(
        _peer_v_kernel,
        grid=(T // tb,),
        in_specs=[
            pl.BlockSpec((tb, PEER_SLOTS), lambda i: (i, 0)),
            pl.BlockSpec((PEER_SLOTS, tb), lambda i: (0, i)),
            tok, tok,
            pl.BlockSpec((ne, 4, LANES), lambda i: (0, 0, 0), pipeline_mode=pl.Buffered(1)),
        ],
        out_specs=[tok, tok],
        out_shape=[jax.ShapeDtypeStruct((T, 4, LANES), F32), jax.ShapeDtypeStruct((T, 4, LANES), F32)],
        scratch_shapes=[pltpu.VMEM((PEER_SLOTS, LANES), F32), pltpu.VMEM((PEER_SLOTS, LANES), F32)]
        + _index_scratch(),
        compiler_params=_cparams(),
        name="peer_v",
    )(idx, a_t, rlo, rhi, tab)


def _halves(a2):
    a4 = a2.reshape(a2.shape[0], 2, 4, LANES)
    return a4[:, 0], a4[:, 1]


def _peer_layer(h2, gain, wq, keys, u, v):
    T = h2.shape[0]
    eye = jnp.eye(PEER_HEADS * 2, dtype=F32)
    kbd = jnp.einsum("pkd,pq->pkqd", keys.reshape(PEER_HEADS * 2, PEER_KEYS, PEER_HALF), eye)
    kbd = kbd.reshape(PEER_HEADS * 2 * PEER_KEYS, D_MODEL).astype(BF16)
    xn, idx, gate_t = _peer_select(h2, gain.reshape(1, D_MODEL), wq.astype(BF16), kbd)
    xlo, xhi = _halves(xn)
    a_t = _peer_u(idx, xlo, xhi, gate_t, _pack_table(u))
    rlo, rhi = _halves(h2)
    olo, ohi = _peer_v(idx, a_t, rlo, rhi, _pack_table(v))
    return jnp.concatenate([olo.reshape(T, D_MODEL // 2), ohi.reshape(T, D_MODEL // 2)], axis=1)


def _gla_layer(x2, B, S, gain, w_in, w_alpha, b_alpha, norm_gain, w_out):
    hk = GLA_HEADS * GLA_DK
    hv = GLA_HEADS * GLA_DV
    w_main = w_in[:, :2 * hk + 2 * hv].astype(BF16)
    w_lr = jnp.pad(w_in[:, 2 * hk + 2 * hv:], ((0, 0), (0, LANES - GLA_RANK))).astype(BF16)
    wa = jnp.pad(w_alpha, ((0, LANES - GLA_RANK), (0, 0))).astype(BF16)
    q, k, v, r, g = _gla_in(x2, gain.reshape(1, D_MODEL), w_main, w_lr, wa, b_alpha.reshape(1, hk))
    sh = lambda a: a.reshape(B, S, a.shape[1])
    og = _gla_scan(sh(q), sh(k), sh(g), sh(v), sh(r), norm_gain.reshape(1, hv))
    return _proj_res(og.reshape(B * S, hv), w_out.astype(BF16), None, x2, "gla_out")


def _swa_layer(h2, pos, B, S, gain, w_in, b_in, q_gain, k_gain, sinks, w_out, b_out):
    nq = SWA_QH * SWA_HD
    hd = SWA_HD
    wq_, wk_, wv_ = w_in[:, :nq], w_in[:, nq:nq + SWA_KVH * hd], w_in[:, nq + SWA_KVH * hd:]
    bq_, bk_, bv_ = b_in[:nq], b_in[nq:nq + SWA_KVH * hd], b_in[nq + SWA_KVH * hd:]

    def dup(a):
        parts = []
        for j in range(SWA_KVH):
            parts += [a[..., j * hd:(j + 1) * hd]] * 2
        return jnp.concatenate(parts, axis=-1)

    w = jnp.concatenate([wq_, dup(wk_), dup(wv_)], axis=1).astype(BF16)
    b = jnp.concatenate([bq_, dup(bk_), dup(bv_)]).reshape(1, -1)
    nk = 2 * SWA_KVH * hd
    pmat = lambda n: (jnp.kron(jnp.eye(n // hd, dtype=F32), jnp.ones((hd, hd), F32)) / hd).astype(BF16)
    inv_freq = ROPE_THETA ** (-jnp.arange(0, ROPE_DIM, 2, dtype=F32) / ROPE_DIM)
    hl = np.arange(LANES) % hd
    freq = jnp.where(hl < ROPE_DIM, inv_freq[hl % (ROPE_DIM // 2)], 0.0).reshape(1, LANES)
    q, k2, v2 = _swa_in(h2, pos.reshape(B * S, 1), gain.reshape(1, D_MODEL), w, b, pmat(nq), pmat(nk),
                        jnp.tile(q_gain, nq // hd).reshape(1, nq), jnp.tile(k_gain, nk // hd).reshape(1, nk), freq)
    o = _swa_attn(sinks, q.reshape(B, S, nq), k2.reshape(B, S, nk), v2.reshape(B, S, nk))
    return _proj_res(o.reshape(B * S, nq), w_out.astype(BF16), b_out.reshape(1, D_MODEL), h2, "swa_out")


def kernel(x, positions, ln_mix, ln_ffn, gla_w_in, gla_w_alpha, gla_b_alpha, gla_norm, gla_w_out, swa_w_in, swa_b_in, swa_q_norm, swa_k_norm, swa_sinks, swa_w_out, swa_b_out, peer_w_q, peer_keys, peer_u, peer_v):
    B, S, D = x.shape
    h = x.reshape(B * S, D)
    h = _gla_layer(h, B, S, ln_mix[0], gla_w_in[0], gla_w_alpha[0], gla_b_alpha[0], gla_norm[0], gla_w_out[0])
    h = _peer_layer(h, ln_ffn[0], peer_w_q[0], peer_keys[0], peer_u[0], peer_v[0])
    h = _swa_layer(h, positions, B, S, ln_mix[1], swa_w_in[0], swa_b_in[0], swa_q_norm[0], swa_k_norm[0],
                   swa_sinks[0], swa_w_out[0], swa_b_out[0])
    h = _peer_layer(h, ln_ffn[1], peer_w_q[1], peer_keys[1], peer_u[1], peer_v[1])
    return h.reshape(B, S, D)
```

```python
import functools

import jax
import jax.numpy as jnp
import numpy as np
from jax import lax
from jax.experimental import pallas as pl
from jax.experimental.pallas import tpu as pltpu

F32 = jnp.float32
BF16 = jnp.bfloat16

D_MODEL = 1024
RMS_EPS = 1e-6

GLA_HEADS = 4
GLA_DK = 128
GLA_DV = 256
GLA_RANK = 16
GLA_TAU = 16.0
GLA_CHUNK = 64

SWA_HD = 64
SWA_QH = 16
SWA_KVH = 2
SWA_GROUP = 8
SWA_BLK = 128
ROPE_THETA = 500000.0
ROPE_DIM = 16

PEER_HEADS = 8
PEER_KEYS = 128
PEER_EXPERTS = PEER_KEYS * PEER_KEYS
PEER_HALF = 64
PEER_TOPK = 16
PEER_SLOTS = PEER_HEADS * PEER_TOPK

LANES = 128
TOKEN_TILE = 256
PEER_TOKENS = 128
IDX_GROUP = 8
TABLE_ROWS = 4
FOLD_STRIDE = 136
VMEM_LIMIT = 48 * 1024 * 1024


def _cparams(n_axes=1):
    return pltpu.CompilerParams(
        dimension_semantics=("arbitrary",) * n_axes, vmem_limit_bytes=VMEM_LIMIT)


def _rms(x, gain):
    ms = jnp.mean(x * x, axis=-1, keepdims=True)
    return x * lax.rsqrt(ms + RMS_EPS) * gain


def _dot(a, b):
    return jnp.dot(a, b, preferred_element_type=F32)


def _dot_nt(a, b):
    return lax.dot_general(a, b, (((1,), (1,)), ((), ())), preferred_element_type=F32)


def _dot_tn(a, b):
    return lax.dot_general(a, b, (((0,), (0,)), ((), ())), preferred_element_type=F32)


def _split2(x):
    hi = x.astype(BF16)
    lo = (x - hi.astype(F32)).astype(BF16)
    return hi, lo


def _split3(x):
    a = x.astype(BF16)
    r = x - a.astype(F32)
    b = r.astype(BF16)
    c = (r - b.astype(F32)).astype(BF16)
    return a, b, c


def _gla_in_kernel(x_ref, gain_ref, w_ref, wlr_ref, wa_ref, ba_ref,
                   q_ref, k_ref, v_ref, r_ref, g_ref):
    hk = GLA_HEADS * GLA_DK
    hv = GLA_HEADS * GLA_DV
    xn = _rms(x_ref[...], gain_ref[...]).astype(BF16)
    q_ref[...] = _dot(xn, w_ref[:, 0:hk])
    k_ref[...] = _dot(xn, w_ref[:, hk:2 * hk])
    v_ref[...] = _dot(xn, w_ref[:, 2 * hk:2 * hk + hv]).astype(BF16)
    r_ref[...] = _dot(xn, w_ref[:, 2 * hk + hv:2 * hk + 2 * hv])
    lr = _dot(xn, wlr_ref[...])
    z = _dot(lr.astype(BF16), wa_ref[...]) + ba_ref[...]
    g_ref[...] = (jnp.minimum(z, 0.0) - jnp.log(1.0 + jnp.exp(-jnp.abs(z)))) / GLA_TAU


def _gla_in(x2, gain, w_main, w_lr, w_alpha, b_alpha):
    T = x2.shape[0]
    hk = GLA_HEADS * GLA_DK
    hv = GLA_HEADS * GLA_DV
    tm = TOKEN_TILE
    row = lambda n: pl.BlockSpec((tm, n), lambda i: (i, 0))
    full = lambda a: pl.BlockSpec(a.shape, lambda i: (0,) * a.ndim)
    return pl.pallas_call(
        _gla_in_kernel,
        grid=(T // tm,),
        in_specs=[row(D_MODEL), full(gain), full(w_main), full(w_lr), full(w_alpha), full(b_alpha)],
        out_specs=[row(hk), row(hk), row(hv), row(hv), row(hk)],
        out_shape=[jax.ShapeDtypeStruct((T, hk), F32), jax.ShapeDtypeStruct((T, hk), F32),
                   jax.ShapeDtypeStruct((T, hv), BF16), jax.ShapeDtypeStruct((T, hv), F32),
                   jax.ShapeDtypeStruct((T, hk), F32)],
        compiler_params=_cparams(),
        name="gla_in",
    )(x2, gain, w_main, w_lr, w_alpha, b_alpha)


def _gla_scan_kernel(q_ref, k_ref, g_ref, v_ref, r_ref, gain_ref, o_ref, st_ref):
    C = GLA_CHUNK
    nb = q_ref.shape[0]

    @pl.when(pl.program_id(0) == 0)
    def _():
        st_ref[...] = jnp.zeros_like(st_ref)

    ri = lax.broadcasted_iota(jnp.int32, (C, C), 0)
    ci = lax.broadcasted_iota(jnp.int32, (C, C), 1)
    tri = ri >= ci
    tri_b = jnp.where(tri, 1.0, 0.0).astype(BF16)
    for b in range(nb):
        g1, g2, g3 = _split3(g_ref[b])
        bc = _dot(tri_b, g1) + _dot(tri_b, g2) + _dot(tri_b, g3)
        eb = jnp.exp(bc)
        enb = jnp.exp(-bc)
        ebl = jnp.exp(bc[C - 1:C, :])
        qd = q_ref[b] * (GLA_DK ** -0.5) * eb
        ki = k_ref[b] * enb
        kf = ki * ebl
        for h in range(GLA_HEADS):
            ks = slice(h * GLA_DK, (h + 1) * GLA_DK)
            vs = slice(h * GLA_DV, (h + 1) * GLA_DV)
            qd_h = qd[:, ks].astype(BF16)
            ki_h = ki[:, ks].astype(BF16)
            kf_h = kf[:, ks].astype(BF16)
            v_h = v_ref[b, :, vs]
            att = jnp.where(tri, _dot_nt(qd_h, ki_h), 0.0).astype(BF16)
            st = st_ref[b * GLA_HEADS + h]
            o = _dot(att, v_h) + _dot_nt(qd_h, st.astype(BF16))
            st_ref[b * GLA_HEADS + h] = st * ebl[:, ks] + _dot_tn(v_h, kf_h)
            ms = jnp.mean(o * o, axis=-1, keepdims=True)
            on = o * lax.rsqrt(ms + RMS_EPS) * gain_ref[:, vs]
            rr = r_ref[b, :, vs]
            o_ref[b, :, vs] = (on * (rr / (1.0 + jnp.exp(-rr)))).astype(BF16)


def _gla_scan(q, k, g, v, r, gain):
    B, S, hk = q.shape
    hv = v.shape[2]
    C = GLA_CHUNK
    blk = lambda n: pl.BlockSpec((B, C, n), lambda c: (0, c, 0))
    return pl.pallas_call(
        _gla_scan_kernel,
        grid=(S // C,),
        in_specs=[blk(hk), blk(hk), blk(hk), blk(hv), blk(hv),
                  pl.BlockSpec(gain.shape, lambda c: (0, 0))],
        out_specs=blk(hv),
        out_shape=jax.ShapeDtypeStruct((B, S, hv), BF16),
        scratch_shapes=[pltpu.VMEM((B * GLA_HEADS, GLA_DV, GLA_DK), F32)],
        compiler_params=_cparams(),
        name="gla_scan",
    )(q, k, g, v, r, gain)


def _proj_res_kernel(a_ref, w_ref, res_ref, o_ref):
    o_ref[...] = res_ref[...] + _dot(a_ref[...], w_ref[...])


def _proj_bias_res_kernel(a_ref, w_ref, b_ref, res_ref, o_ref):
    o_ref[...] = res_ref[...] + (_dot(a_ref[...], w_ref[...]) + b_ref[...])


def _proj_res(a, w, bias, res, name):
    T, K = a.shape
    N = w.shape[1]
    tm = TOKEN_TILE
    row = lambda n: pl.BlockSpec((tm, n), lambda i: (i, 0))
    wspec = pl.BlockSpec((K, N), lambda i: (0, 0))
    if bias is None:
        body, specs, args = _proj_res_kernel, [row(K), wspec, row(N)], (a, w, res)
    else:
        body = _proj_bias_res_kernel
        specs = [row(K), wspec, pl.BlockSpec((1, N), lambda i: (0, 0)), row(N)]
        args = (a, w, bias, res)
    return pl.pallas_call(
        body,
        grid=(T // tm,),
        in_specs=specs,
        out_specs=row(N),
        out_shape=jax.ShapeDtypeStruct((T, N), F32),
        compiler_params=_cparams(),
        name=name,
    )(*args)


def _swa_in_kernel(h_ref, pos_ref, gain_ref, w_ref, b_ref, pq_ref, pk_ref, qg_ref, kg_ref,
                   freq_ref, q_ref, k_ref, v_ref):
    nq = SWA_QH * SWA_HD
    nk = 2 * SWA_KVH * SWA_HD
    xn = _rms(h_ref[...], gain_ref[...]).astype(BF16)
    proj = _dot(xn, w_ref[...]) + b_ref[...]
    q = proj[:, 0:nq]
    k = proj[:, nq:nq + nk]
    v = proj[:, nq + nk:nq + 2 * nk]

    def headnorm(t, p_ref, gain):
        s1, s2 = _split2(t * t)
        ms = _dot(s1, p_ref[...]) + _dot(s2, p_ref[...])
        return t * lax.rsqrt(ms + RMS_EPS) * gain

    q = headnorm(q, pq_ref, qg_ref[...])
    k = headnorm(k, pk_ref, kg_ref[...])

    ang = pos_ref[...].astype(F32) * freq_ref[...]
    cs = jnp.cos(ang)
    sn = jnp.sin(ang)
    hl = lax.broadcasted_iota(jnp.int32, ang.shape, 1) & (SWA_HD - 1)
    half = ROPE_DIM // 2
    s_lo = jnp.where(hl < half, -sn, 0.0)
    s_hi = jnp.where((hl >= half) & (hl < ROPE_DIM), sn, 0.0)

    def rope(t):
        outs = []
        for c in range(t.shape[1] // LANES):
            x = t[:, c * LANES:(c + 1) * LANES]
            x_up = pltpu.roll(x, LANES - half, axis=1)
            x_dn = pltpu.roll(x, half, axis=1)
            outs.append(x * cs + x_up * s_lo + x_dn * s_hi)
        return jnp.concatenate(outs, axis=1)

    q_ref[...] = (rope(q) * (SWA_HD ** -0.5)).astype(BF16)
    k_ref[...] = rope(k).astype(BF16)
    v_ref[...] = v.astype(BF16)


def _swa_in(h2, pos, gain, w, b, pq, pk, qg, kg, freq):
    T = h2.shape[0]
    nq = SWA_QH * SWA_HD
    nk = 2 * SWA_KVH * SWA_HD
    tm = TOKEN_TILE
    row = lambda n: pl.BlockSpec((tm, n), lambda i: (i, 0))
    full = lambda a: pl.BlockSpec(a.shape, lambda i: (0,) * a.ndim)
    return pl.pallas_call(
        _swa_in_kernel,
        grid=(T // tm,),
        in_specs=[row(D_MODEL), row(1), full(gain), full(w), full(b), full(pq), full(pk),
                  full(qg), full(kg), full(freq)],
        out_specs=[row(nq), row(nk), row(nk)],
        out_shape=[jax.ShapeDtypeStruct((T, nq), BF16), jax.ShapeDtypeStruct((T, nk), BF16),
                   jax.ShapeDtypeStruct((T, nk), BF16)],
        compiler_params=_cparams(),
        name="swa_in",
    )(h2, pos, gain, w, b, pq, pk, qg, kg, freq)


def _swa_attn_kernel(sink_ref, q_ref, kc_ref, kp_ref, vc_ref, vp_ref, o_ref):
    n = pl.program_id(1)
    blk = SWA_BLK
    kk = jnp.concatenate([kp_ref[...], kc_ref[...]], axis=0)
    vv = jnp.concatenate([vp_ref[...], vc_ref[...]], axis=0)
    qi = lax.broadcasted_iota(jnp.int32, (blk, 2 * blk), 0)
    ki = lax.broadcasted_iota(jnp.int32, (blk, 2 * blk), 1)
    rel = qi + blk - ki
    first_key = jnp.where(n > 0, 0, blk)
    mask = (rel >= 0) & (rel < blk) & (ki >= first_key)
    lane = lax.broadcasted_iota(jnp.int32, (blk, LANES), 1)
    first = lane < SWA_HD
    for j in range(SWA_KVH):
        k2 = kk[:, j * LANES:(j + 1) * LANES]
        v2 = vv[:, j * LANES:(j + 1) * LANES]
        for gp in range(SWA_GROUP // 2):
            c0 = j * (SWA_GROUP * SWA_HD) + gp * LANES
            qp = q_ref[:, c0:c0 + LANES]
            outs = []
            for hf in range(2):
                qm = jnp.where(first if hf == 0 else ~first, qp, jnp.zeros_like(qp))
                s = jnp.where(mask, _dot_nt(qm, k2), -jnp.inf)
                sink = sink_ref[j * SWA_GROUP + 2 * gp + hf]
                m = jnp.maximum(jnp.max(s, axis=-1, keepdims=True), sink)
                p = jnp.exp(s - m)
                denom = jnp.sum(p, axis=-1, keepdims=True) + jnp.exp(sink - m)
                outs.append(_dot((p / denom).astype(BF16), v2))
            o_ref[:, c0:c0 + LANES] = jnp.where(first, outs[0], outs[1]).astype(BF16)


def _swa_attn(sinks, q, k2, v2):
    B, S, nq = q.shape
    nk = k2.shape[2]
    blk = SWA_BLK
    cur = lambda n_: pl.BlockSpec((None, blk, n_), lambda b, n: (b, n, 0))
    prev = lambda n_: pl.BlockSpec((None, blk, n_), lambda b, n: (b, jnp.maximum(n - 1, 0), 0))
    return pl.pallas_call(
        _swa_attn_kernel,
        grid=(B, S // blk),
        in_specs=[pl.BlockSpec(memory_space=pltpu.SMEM), cur(nq), cur(nk), prev(nk), cur(nk), prev(nk)],
        out_specs=cur(nq),
        out_shape=jax.ShapeDtypeStruct((B, S, nq), BF16),
        compiler_params=_cparams(2),
        name="swa_attn",
    )(sinks, q, k2, k2, v2, v2)


def _top16_rows(a, nrows):
    rid = lax.broadcasted_iota(jnp.int32, a.shape, 0)
    vals, ids = [], []
    for _ in range(PEER_TOPK):
        m = jnp.max(a, axis=0, keepdims=True)
        am = jnp.min(jnp.where(a == m, rid, nrows), axis=0, keepdims=True)
        vals.append(m)
        ids.append(am)
        a = jnp.where(rid == am, -jnp.inf, a)
    return jnp.concatenate(vals, axis=0), jnp.concatenate(ids, axis=0)


def _staircase_blocks():
    K = PEER_TOPK
    blocks = []
    for i in range(4):
        nj = K // (i + 1)
        for j0 in range(0, nj, 8):
            blocks.append(("j", i, j0, [j0 + r < nj for r in range(8)]))
    for j in range(3):
        for i0 in (0, 8):
            keep = [(i0 + r >= 4) and ((i0 + r + 1) * (j + 1) <= K) for r in range(8)]
            if any(keep):
                blocks.append(("i", j, i0, keep))
    return blocks


_STAIR = _staircase_blocks()


def _peer_select_kernel(h_ref, gain_ref, wq_ref, kbd_ref, xn_ref, idx_ref, gate_ref):
    K = PEER_TOPK
    tm = h_ref.shape[0]
    xn = _rms(h_ref[...], gain_ref[...])
    xn_ref[...] = xn
    qv = _dot(xn.astype(BF16), wq_ref[...]).astype(BF16)
    st = _dot_nt(kbd_ref[...], qv)
    sub = lax.broadcasted_iota(jnp.int32, (8, tm), 0)
    experts = []
    for h in range(PEER_HEADS):
        r0 = 2 * h * PEER_KEYS
        v1, i1 = _top16_rows(st[r0:r0 + PEER_KEYS], PEER_KEYS)
        v2, i2 = _top16_rows(st[r0 + PEER_KEYS:r0 + 2 * PEER_KEYS], PEER_KEYS)
        cv, cc = [], []
        for kind, fixed, start, keep in _STAIR:
            keepm = functools.reduce(jnp.logical_or, [sub == r for r in range(8) if keep[r]])
            if kind == "j":
                val = v1[fixed:fixed + 1] + v2[start:start + 8]
                code = fixed * K + start + sub
            else:
                val = v1[start:start + 8] + v2[fixed:fixed + 1]
                code = (start + sub) * K + fixed
            cv.append(jnp.where(keepm, val, -jnp.inf))
            cc.append(code)
        cand = jnp.concatenate(cv, axis=0)
        code = jnp.concatenate(cc, axis=0)
        scs, cis = [], []
        for _ in range(K):
            m = jnp.max(cand, axis=0, keepdims=True)
            cm = jnp.min(jnp.where(cand == m, code, K * K), axis=0, keepdims=True)
            scs.append(m)
            cis.append(cm)
            cand = jnp.where(code == cm, -jnp.inf, cand)
        sc = jnp.concatenate(scs, axis=0)
        ci = jnp.concatenate(cis, axis=0)
        chi = ci >> 4
        clo = ci & (K - 1)
        e1 = jnp.zeros_like(ci)
        e2 = jnp.zeros_like(ci)
        for i in range(K):
            e1 = jnp.where(chi == i, i1[i:i + 1], e1)
            e2 = jnp.where(clo == i, i2[i:i + 1], e2)
        experts.append((e1 * PEER_KEYS + e2) * TABLE_ROWS)
        ex = jnp.exp(sc - sc[0:1])
        gate_ref[h * K:(h + 1) * K, :] = ex / jnp.sum(ex, axis=0, keepdims=True)
    idx_ref[...] = jnp.concatenate(experts, axis=0).T


def _peer_select(h2, gain, wq, kbd):
    T = h2.shape[0]
    tm = TOKEN_TILE
    full = lambda a: pl.BlockSpec(a.shape, lambda i: (0,) * a.ndim)
    return pl.pallas_call(
        _peer_select_kernel,
        grid=(T // tm,),
        in_specs=[pl.BlockSpec((tm, D_MODEL), lambda i: (i, 0)), full(gain), full(wq), full(kbd)],
        out_specs=[pl.BlockSpec((tm, D_MODEL), lambda i: (i, 0)),
                   pl.BlockSpec((tm, PEER_SLOTS), lambda i: (i, 0)),
                   pl.BlockSpec((PEER_SLOTS, tm), lambda i: (0, i))],
        out_shape=[jax.ShapeDtypeStruct((T, D_MODEL), F32),
                   jax.ShapeDtypeStruct((T, PEER_SLOTS), jnp.int32),
                   jax.ShapeDtypeStruct((PEER_SLOTS, T), F32)],
        compiler_params=_cparams(),
        name="peer_select",
    )(h2, gain, wq, kbd)


def _pack_table(w):
    wb = w.astype(BF16)
    half = D_MODEL // 2
    lo = lax.bitcast_convert_type(wb[:, :half], jnp.uint16).astype(jnp.uint32)
    hi = lax.bitcast_convert_type(wb[:, half:], jnp.uint16).astype(jnp.uint32)
    return (lo | (hi << 16)).reshape(w.shape[0], 4, LANES)


def _unpack(w):
    lo = lax.bitcast_convert_type(w << 16, F32)
    hi = lax.bitcast_convert_type(w & jnp.uint32(0xFFFF0000), F32)
    return lo, hi


def _expert_rows(tab_ref, row):
    return tab_ref[pl.ds(pl.multiple_of(row, TABLE_ROWS), TABLE_ROWS), :]


def _stream_index_groups(idx_ref, ibufs, sems, process):
    ng = PEER_TOKENS // IDX_GROUP

    def copy(g, slot):
        rows = pl.ds(pl.multiple_of(g * IDX_GROUP, IDX_GROUP), IDX_GROUP)
        return pltpu.make_async_copy(idx_ref.at[rows], ibufs[slot], sems.at[slot])

    copy(0, 0).start()

    def pair(p, carry):
        g = 2 * p
        copy(g + 1, 1).start()
        copy(g, 0).wait()
        process(g, ibufs[0])
        nxt = jnp.minimum(g + 2, ng - 1)
        copy(nxt, 0).start()
        copy(g + 1, 1).wait()
        process(g + 1, ibufs[1])
        return carry

    lax.fori_loop(0, ng // 2, pair, 0)
    copy(ng - 1, 0).wait()


def _index_scratch():
    return [pltpu.SMEM((IDX_GROUP, PEER_SLOTS), jnp.int32), pltpu.SMEM((IDX_GROUP, PEER_SLOTS), jnp.int32),
            pltpu.SemaphoreType.DMA((2,))]


def _peer_u_kernel(idx_ref, xlo_ref, xhi_ref, gate_ref, tab_ref, a_ref,
                   s0_ref, s1_ref, ib0_ref, ib1_ref, sems):
    tb = PEER_TOKENS
    ss = FOLD_STRIDE
    lane = lax.broadcasted_iota(jnp.int32, (8, tb), 1)
    folds = (s0_ref, s1_ref)

    def gather(t, j, ibuf, s_ref):
        xlo = xlo_ref[t]
        xhi = xhi_ref[t]
        for k in range(PEER_SLOTS):
            lo, hi = _unpack(_expert_rows(tab_ref, ibuf[j, k]))
            s_ref[pl.ds(k, 4, stride=ss), :] = lo * xlo + hi * xhi

    def fold(t, s_ref):
        sel = lane == t
        for j in range(PEER_SLOTS // 8):
            r = s_ref[pl.ds(8 * j, 8), :]
            for s in range(1, 4):
                r = r + s_ref[pl.ds(s * ss + 8 * j, 8), :]
            tot = jnp.sum(r, axis=1, keepdims=True)
            a_ref[pl.ds(8 * j, 8), :] = jnp.where(sel, tot, a_ref[pl.ds(8 * j, 8), :])

    def process(g, ibuf):
        for j in range(IDX_GROUP):
            t = g * IDX_GROUP + j
            gather(t, j, ibuf, folds[j % 2])
            fold(t - 1, folds[(j + 1) % 2])

    a_ref[...] = jnp.zeros_like(a_ref)
    s1_ref[...] = jnp.zeros_like(s1_ref)
    _stream_index_groups(idx_ref, (ib0_ref, ib1_ref), sems, process)
    fold(tb - 1, folds[(tb - 1) % 2])
    z = a_ref[...]
    a_ref[...] = 0.5 * z * (1.0 + lax.erf(z * (2.0 ** -0.5))) * gate_ref[...]


def _peer_u(idx, xlo, xhi, gate_t, tab):
    T = idx.shape[0]
    tb = PEER_TOKENS
    ne = tab.shape[0]
    return pl.pallas_call(
        _peer_u_kernel,
        grid=(T // tb,),
        in_specs=[
            pl.BlockSpec((tb, PEER_SLOTS), lambda i: (i, 0)),
            pl.BlockSpec((tb, 4, LANES), lambda i: (i, 0, 0)),
            pl.BlockSpec((tb, 4, LANES), lambda i: (i, 0, 0)),
            pl.BlockSpec((PEER_SLOTS, tb), lambda i: (0, i)),
            pl.BlockSpec((ne * 4, LANES), lambda i: (0, 0), pipeline_mode=pl.Buffered(1)),
        ],
        out_specs=pl.BlockSpec((PEER_SLOTS, tb), lambda i: (0, i)),
        out_shape=jax.ShapeDtypeStruct((PEER_SLOTS, T), F32),
        scratch_shapes=[pltpu.VMEM((4 * FOLD_STRIDE, LANES), F32),
                        pltpu.VMEM((4 * FOLD_STRIDE, LANES), F32)] + _index_scratch(),
        compiler_params=_cparams(),
        name="peer_u",
    )(idx, xlo, xhi, gate_t, tab.reshape(ne * 4, LANES))


def _peer_v_kernel(idx_ref, a_ref, rlo_ref, rhi_ref, tab_ref, olo_ref, ohi_ref,
                   b0_ref, b1_ref, ib0_ref, ib1_ref, sems):
    tb = PEER_TOKENS
    lane = lax.broadcasted_iota(jnp.int32, (PEER_SLOTS, tb), 1)
    bcast = (b0_ref, b1_ref)

    def prep(t, b_ref):
        col = jnp.sum(jnp.where(lane == t, a_ref[...], 0.0), axis=1, keepdims=True)
        b_ref[...] = jnp.broadcast_to(col, (PEER_SLOTS, LANES))

    def accum(t, j, ibuf, b_ref):
        acc_lo = [rlo_ref[t], jnp.zeros((4, LANES), F32)]
        acc_hi = [rhi_ref[t], jnp.zeros((4, LANES), F32)]
        for k in range(PEER_SLOTS):
            lo, hi = _unpack(_expert_rows(tab_ref, ibuf[j, k]))
            ab = jnp.broadcast_to(b_ref[pl.ds(k, 1), :], (4, LANES))
            acc_lo[k % 2] = acc_lo[k % 2] + ab * lo
            acc_hi[k % 2] = acc_hi[k % 2] + ab * hi
        olo_ref[t] = acc_lo[0] + acc_lo[1]
        ohi_ref[t] = acc_hi[0] + acc_hi[1]

    def process(g, ibuf):
        for j in range(IDX_GROUP):
            t = g * IDX_GROUP + j
            prep(jnp.minimum(t + 1, tb - 1), bcast[(j + 1) % 2])
            accum(t, j, ibuf, bcast[j % 2])

    prep(0, b0_ref)
    _stream_index_groups(idx_ref, (ib0_ref, ib1_ref), sems, process)


def _peer_v(idx, a_t, rlo, rhi, tab):
    T = idx.shape[0]
    tb = PEER_TOKENS
    ne = tab.shape[0]
    tok = pl.BlockSpec((tb, 4, LANES), lambda i: (i, 0, 0))
    return pl.pallas_call(
        _peer_v_kernel,
        grid=(T // tb,),
        in_specs=[
            pl.BlockSpec((tb, PEER_SLOTS), lambda i: (i, 0)),
            pl.BlockSpec((PEER_SLOTS, tb), lambda i: (0, i)),
            tok, tok,
            pl.BlockSpec((ne * 4, LANES), lambda i: (0, 0), pipeline_mode=pl.Buffered(1)),
        ],
        out_specs=[tok, tok],
        out_shape=[jax.ShapeDtypeStruct((T, 4, LANES), F32), jax.ShapeDtypeStruct((T, 4, LANES), F32)],
        scratch_shapes=[pltpu.VMEM((PEER_SLOTS, LANES), F32), pltpu.VMEM((PEER_SLOTS, LANES), F32)]
        + _index_scratch(),
        compiler_params=_cparams(),
        name="peer_v",
    )(idx, a_t, rlo, rhi, tab.reshape(ne * 4, LANES))


def _halves(a2):
    a4 = a2.reshape(a2.shape[0], 2, 4, LANES)
    return a4[:, 0], a4[:, 1]


def _peer_layer(h2, gain, wq, keys, u, v):
    T = h2.shape[0]
    eye = jnp.eye(PEER_HEADS * 2, dtype=F32)
    kbd = jnp.einsum("pkd,pq->pkqd", keys.reshape(PEER_HEADS * 2, PEER_KEYS, PEER_HALF), eye)
    kbd = kbd.reshape(PEER_HEADS * 2 * PEER_KEYS, D_MODEL).astype(BF16)
    xn, idx, gate_t = _peer_select(h2, gain.reshape(1, D_MODEL), wq.astype(BF16), kbd)
    xlo, xhi = _halves(xn)
    a_t = _peer_u(idx, xlo, xhi, gate_t, _pack_table(u))
    rlo, rhi = _halves(h2)
    olo, ohi = _peer_v(idx, a_t, rlo, rhi, _pack_table(v))
    return jnp.concatenate([olo.reshape(T, D_MODEL // 2), ohi.reshape(T, D_MODEL // 2)], axis=1)


def _gla_layer(x2, B, S, gain, w_in, w_alpha, b_alpha, norm_gain, w_out):
    hk = GLA_HEADS * GLA_DK
    hv = GLA_HEADS * GLA_DV
    w_main = w_in[:, :2 * hk + 2 * hv].astype(BF16)
    w_lr = jnp.pad(w_in[:, 2 * hk + 2 * hv:], ((0, 0), (0, LANES - GLA_RANK))).astype(BF16)
    wa = jnp.pad(w_alpha, ((0, LANES - GLA_RANK), (0, 0))).astype(BF16)
    q, k, v, r, g = _gla_in(x2, gain.reshape(1, D_MODEL), w_main, w_lr, wa, b_alpha.reshape(1, hk))
    sh = lambda a: a.reshape(B, S, a.shape[1])
    og = _gla_scan(sh(q), sh(k), sh(g), sh(v), sh(r), norm_gain.reshape(1, hv))
    return _proj_res(og.reshape(B * S, hv), w_out.astype(BF16), None, x2, "gla_out")


def _swa_layer(h2, pos, B, S, gain, w_in, b_in, q_gain, k_gain, sinks, w_out, b_out):
    nq = SWA_QH * SWA_HD
    hd = SWA_HD
    wq_, wk_, wv_ = w_in[:, :nq], w_in[:, nq:nq + SWA_KVH * hd], w_in[:, nq + SWA_KVH * hd:]
    bq_, bk_, bv_ = b_in[:nq], b_in[nq:nq + SWA_KVH * hd], b_in[nq + SWA_KVH * hd:]

    def dup(a):
        parts = []
        for j in range(SWA_KVH):
            parts += [a[..., j * hd:(j + 1) * hd]] * 2
        return jnp.concatenate(parts, axis=-1)

    w = jnp.concatenate([wq_, dup(wk_), dup(wv_)], axis=1).astype(BF16)
    b = jnp.concatenate([bq_, dup(bk_), dup(bv_)]).reshape(1, -1)
    nk = 2 * SWA_KVH * hd
    pmat = lambda n: (jnp.kron(jnp.eye(n // hd, dtype=F32), jnp.ones((hd, hd), F32)) / hd).astype(BF16)
    inv_freq = ROPE_THETA ** (-jnp.arange(0, ROPE_DIM, 2, dtype=F32) / ROPE_DIM)
    hl = np.arange(LANES) % hd
    freq = jnp.where(hl < ROPE_DIM, inv_freq[hl % (ROPE_DIM // 2)], 0.0).reshape(1, LANES)
    q, k2, v2 = _swa_in(h2, pos.reshape(B * S, 1), gain.reshape(1, D_MODEL), w, b, pmat(nq), pmat(nk),
                        jnp.tile(q_gain, nq // hd).reshape(1, nq), jnp.tile(k_gain, nk // hd).reshape(1, nk), freq)
    o = _swa_attn(sinks, q.reshape(B, S, nq), k2.reshape(B, S, nk), v2.reshape(B, S, nk))
    return _proj_res(o.reshape(B * S, nq), w_out.astype(BF16), b_out.reshape(1, D_MODEL), h2, "swa_out")


def kernel(x, positions, ln_mix, ln_ffn, gla_w_in, gla_w_alpha, gla_b_alpha, gla_norm, gla_w_out, swa_w_in, swa_b_in, swa_q_norm, swa_k_norm, swa_sinks, swa_w_out, swa_b_out, peer_w_q, peer_keys, peer_u, peer_v):
    B, S, D = x.shape
    h = x.reshape(B * S, D)
    h = _gla_layer(h, B, S, ln_mix[0], gla_w_in[0], gla_w_alpha[0], gla_b_alpha[0], gla_norm[0], gla_w_out[0])
    h = _peer_layer(h, ln_ffn[0], peer_w_q[0], peer_keys[0], peer_u[0], peer_v[0])
    h = _swa_layer(h, positions, B, S, ln_mix[1], swa_w_in[0], swa_b_in[0], swa_q_norm[0], swa_k_norm[0],
                   swa_sinks[0], swa_w_out[0], swa_b_out[0])
    h = _peer_layer(h, ln_ffn[1], peer_w_q[1], peer_keys[1], peer_u[1], peer_v[1])
    return h.reshape(B, S, D)
```

```python
import functools

import jax
import jax.numpy as jnp
import numpy as np
from jax import lax
from jax.experimental import pallas as pl
from jax.experimental.pallas import tpu as pltpu

F32 = jnp.float32
BF16 = jnp.bfloat16

D_MODEL = 1024
RMS_EPS = 1e-6

GLA_HEADS = 4
GLA_DK = 128
GLA_DV = 256
GLA_RANK = 16
GLA_TAU = 16.0
GLA_CHUNK = 64

SWA_HD = 64
SWA_QH = 16
SWA_KVH = 2
SWA_GROUP = 8
SWA_BLK = 128
ROPE_THETA = 500000.0
ROPE_DIM = 16

PEER_HEADS = 8
PEER_KEYS = 128
PEER_EXPERTS = PEER_KEYS * PEER_KEYS
PEER_HALF = 64
PEER_TOPK = 16
PEER_SLOTS = PEER_HEADS * PEER_TOPK

LANES = 128
TOKEN_TILE = 256
PEER_TOKENS = 128
IDX_GROUP = 8
TABLE_ROWS = 4
V_CHAINS = 4
FOLD_STRIDE = 136
VMEM_LIMIT = 48 * 1024 * 1024


def _cparams(n_axes=1):
    return pltpu.CompilerParams(
        dimension_semantics=("arbitrary",) * n_axes, vmem_limit_bytes=VMEM_LIMIT)


def _rms(x, gain):
    ms = jnp.mean(x * x, axis=-1, keepdims=True)
    return x * lax.rsqrt(ms + RMS_EPS) * gain


def _dot(a, b):
    return jnp.dot(a, b, preferred_element_type=F32)


def _dot_nt(a, b):
    return lax.dot_general(a, b, (((1,), (1,)), ((), ())), preferred_element_type=F32)


def _dot_tn(a, b):
    return lax.dot_general(a, b, (((0,), (0,)), ((), ())), preferred_element_type=F32)


def _split2(x):
    hi = x.astype(BF16)
    lo = (x - hi.astype(F32)).astype(BF16)
    return hi, lo


def _split3(x):
    a = x.astype(BF16)
    r = x - a.astype(F32)
    b = r.astype(BF16)
    c = (r - b.astype(F32)).astype(BF16)
    return a, b, c


def _gla_in_kernel(x_ref, gain_ref, w_ref, wlr_ref, wa_ref, ba_ref,
                   q_ref, k_ref, v_ref, r_ref, g_ref):
    hk = GLA_HEADS * GLA_DK
    hv = GLA_HEADS * GLA_DV
    xn = _rms(x_ref[...], gain_ref[...]).astype(BF16)
    q_ref[...] = _dot(xn, w_ref[:, 0:hk])
    k_ref[...] = _dot(xn, w_ref[:, hk:2 * hk])
    v_ref[...] = _dot(xn, w_ref[:, 2 * hk:2 * hk + hv]).astype(BF16)
    r_ref[...] = _dot(xn, w_ref[:, 2 * hk + hv:2 * hk + 2 * hv])
    lr = _dot(xn, wlr_ref[...])
    z = _dot(lr.astype(BF16), wa_ref[...]) + ba_ref[...]
    g_ref[...] = (jnp.minimum(z, 0.0) - jnp.log(1.0 + jnp.exp(-jnp.abs(z)))) / GLA_TAU


def _gla_in(x2, gain, w_main, w_lr, w_alpha, b_alpha):
    T = x2.shape[0]
    hk = GLA_HEADS * GLA_DK
    hv = GLA_HEADS * GLA_DV
    tm = TOKEN_TILE
    row = lambda n: pl.BlockSpec((tm, n), lambda i: (i, 0))
    full = lambda a: pl.BlockSpec(a.shape, lambda i: (0,) * a.ndim)
    return pl.pallas_call(
        _gla_in_kernel,
        grid=(T // tm,),
        in_specs=[row(D_MODEL), full(gain), full(w_main), full(w_lr), full(w_alpha), full(b_alpha)],
        out_specs=[row(hk), row(hk), row(hv), row(hv), row(hk)],
        out_shape=[jax.ShapeDtypeStruct((T, hk), F32), jax.ShapeDtypeStruct((T, hk), F32),
                   jax.ShapeDtypeStruct((T, hv), BF16), jax.ShapeDtypeStruct((T, hv), F32),
                   jax.ShapeDtypeStruct((T, hk), F32)],
        compiler_params=_cparams(),
        name="gla_in",
    )(x2, gain, w_main, w_lr, w_alpha, b_alpha)


def _gla_scan_kernel(q_ref, k_ref, g_ref, v_ref, r_ref, gain_ref, o_ref, st_ref):
    C = GLA_CHUNK
    nb = q_ref.shape[0]

    @pl.when(pl.program_id(0) == 0)
    def _():
        st_ref[...] = jnp.zeros_like(st_ref)

    ri = lax.broadcasted_iota(jnp.int32, (C, C), 0)
    ci = lax.broadcasted_iota(jnp.int32, (C, C), 1)
    tri = ri >= ci
    tri_b = jnp.where(tri, 1.0, 0.0).astype(BF16)
    for b in range(nb):
        g1, g2, g3 = _split3(g_ref[b])
        bc = _dot(tri_b, g1) + _dot(tri_b, g2) + _dot(tri_b, g3)
        eb = jnp.exp(bc)
        enb = jnp.exp(-bc)
        ebl = jnp.exp(bc[C - 1:C, :])
        qd = q_ref[b] * (GLA_DK ** -0.5) * eb
        ki = k_ref[b] * enb
        kf = ki * ebl
        for h in range(GLA_HEADS):
            ks = slice(h * GLA_DK, (h + 1) * GLA_DK)
            vs = slice(h * GLA_DV, (h + 1) * GLA_DV)
            qd_h = qd[:, ks].astype(BF16)
            ki_h = ki[:, ks].astype(BF16)
            kf_h = kf[:, ks].astype(BF16)
            v_h = v_ref[b, :, vs]
            att = jnp.where(tri, _dot_nt(qd_h, ki_h), 0.0).astype(BF16)
            st = st_ref[b * GLA_HEADS + h]
            o = _dot(att, v_h) + _dot_nt(qd_h, st.astype(BF16))
            st_ref[b * GLA_HEADS + h] = st * ebl[:, ks] + _dot_tn(v_h, kf_h)
            ms = jnp.mean(o * o, axis=-1, keepdims=True)
            on = o * lax.rsqrt(ms + RMS_EPS) * gain_ref[:, vs]
            rr = r_ref[b, :, vs]
            o_ref[b, :, vs] = (on * (rr / (1.0 + jnp.exp(-rr)))).astype(BF16)


def _gla_scan(q, k, g, v, r, gain):
    B, S, hk = q.shape
    hv = v.shape[2]
    C = GLA_CHUNK
    blk = lambda n: pl.BlockSpec((B, C, n), lambda c: (0, c, 0))
    return pl.pallas_call(
        _gla_scan_kernel,
        grid=(S // C,),
        in_specs=[blk(hk), blk(hk), blk(hk), blk(hv), blk(hv),
                  pl.BlockSpec(gain.shape, lambda c: (0, 0))],
        out_specs=blk(hv),
        out_shape=jax.ShapeDtypeStruct((B, S, hv), BF16),
        scratch_shapes=[pltpu.VMEM((B * GLA_HEADS, GLA_DV, GLA_DK), F32)],
        compiler_params=_cparams(),
        name="gla_scan",
    )(q, k, g, v, r, gain)


def _proj_res_kernel(a_ref, w_ref, res_ref, o_ref):
    o_ref[...] = res_ref[...] + _dot(a_ref[...], w_ref[...])


def _proj_bias_res_kernel(a_ref, w_ref, b_ref, res_ref, o_ref):
    o_ref[...] = res_ref[...] + (_dot(a_ref[...], w_ref[...]) + b_ref[...])


def _proj_res(a, w, bias, res, name):
    T, K = a.shape
    N = w.shape[1]
    tm = TOKEN_TILE
    row = lambda n: pl.BlockSpec((tm, n), lambda i: (i, 0))
    wspec = pl.BlockSpec((K, N), lambda i: (0, 0))
    if bias is None:
        body, specs, args = _proj_res_kernel, [row(K), wspec, row(N)], (a, w, res)
    else:
        body = _proj_bias_res_kernel
        specs = [row(K), wspec, pl.BlockSpec((1, N), lambda i: (0, 0)), row(N)]
        args = (a, w, bias, res)
    return pl.pallas_call(
        body,
        grid=(T // tm,),
        in_specs=specs,
        out_specs=row(N),
        out_shape=jax.ShapeDtypeStruct((T, N), F32),
        compiler_params=_cparams(),
        name=name,
    )(*args)


def _swa_in_kernel(h_ref, pos_ref, gain_ref, w_ref, b_ref, pq_ref, pk_ref, qg_ref, kg_ref,
                   freq_ref, q_ref, k_ref, v_ref):
    nq = SWA_QH * SWA_HD
    nk = 2 * SWA_KVH * SWA_HD
    xn = _rms(h_ref[...], gain_ref[...]).astype(BF16)
    proj = _dot(xn, w_ref[...]) + b_ref[...]
    q = proj[:, 0:nq]
    k = proj[:, nq:nq + nk]
    v = proj[:, nq + nk:nq + 2 * nk]

    def headnorm(t, p_ref, gain):
        s1, s2 = _split2(t * t)
        ms = _dot(s1, p_ref[...]) + _dot(s2, p_ref[...])
        return t * lax.rsqrt(ms + RMS_EPS) * gain

    q = headnorm(q, pq_ref, qg_ref[...])
    k = headnorm(k, pk_ref, kg_ref[...])

    ang = pos_ref[...].astype(F32) * freq_ref[...]
    cs = jnp.cos(ang)
    sn = jnp.sin(ang)
    hl = lax.broadcasted_iota(jnp.int32, ang.shape, 1) & (SWA_HD - 1)
    half = ROPE_DIM // 2
    s_lo = jnp.where(hl < half, -sn, 0.0)
    s_hi = jnp.where((hl >= half) & (hl < ROPE_DIM), sn, 0.0)

    def rope(t):
        outs = []
        for c in range(t.shape[1] // LANES):
            x = t[:, c * LANES:(c + 1) * LANES]
            x_up = pltpu.roll(x, LANES - half, axis=1)
            x_dn = pltpu.roll(x, half, axis=1)
            outs.append(x * cs + x_up * s_lo + x_dn * s_hi)
        return jnp.concatenate(outs, axis=1)

    q_ref[...] = (rope(q) * (SWA_HD ** -0.5)).astype(BF16)
    k_ref[...] = rope(k).astype(BF16)
    v_ref[...] = v.astype(BF16)


def _swa_in(h2, pos, gain, w, b, pq, pk, qg, kg, freq):
    T = h2.shape[0]
    nq = SWA_QH * SWA_HD
    nk = 2 * SWA_KVH * SWA_HD
    tm = TOKEN_TILE
    row = lambda n: pl.BlockSpec((tm, n), lambda i: (i, 0))
    full = lambda a: pl.BlockSpec(a.shape, lambda i: (0,) * a.ndim)
    return pl.pallas_call(
        _swa_in_kernel,
        grid=(T // tm,),
        in_specs=[row(D_MODEL), row(1), full(gain), full(w), full(b), full(pq), full(pk),
                  full(qg), full(kg), full(freq)],
        out_specs=[row(nq), row(nk), row(nk)],
        out_shape=[jax.ShapeDtypeStruct((T, nq), BF16), jax.ShapeDtypeStruct((T, nk), BF16),
                   jax.ShapeDtypeStruct((T, nk), BF16)],
        compiler_params=_cparams(),
        name="swa_in",
    )(h2, pos, gain, w, b, pq, pk, qg, kg, freq)


def _swa_attn_kernel(sink_ref, q_ref, kc_ref, kp_ref, vc_ref, vp_ref, o_ref):
    n = pl.program_id(1)
    blk = SWA_BLK
    kk = jnp.concatenate([kp_ref[...], kc_ref[...]], axis=0)
    vv = jnp.concatenate([vp_ref[...], vc_ref[...]], axis=0)
    qi = lax.broadcasted_iota(jnp.int32, (blk, 2 * blk), 0)
    ki = lax.broadcasted_iota(jnp.int32, (blk, 2 * blk), 1)
    rel = qi + blk - ki
    first_key = jnp.where(n > 0, 0, blk)
    mask = (rel >= 0) & (rel < blk) & (ki >= first_key)
    lane = lax.broadcasted_iota(jnp.int32, (blk, LANES), 1)
    first = lane < SWA_HD
    for j in range(SWA_KVH):
        k2 = kk[:, j * LANES:(j + 1) * LANES]
        v2 = vv[:, j * LANES:(j + 1) * LANES]
        for gp in range(SWA_GROUP // 2):
            c0 = j * (SWA_GROUP * SWA_HD) + gp * LANES
            qp = q_ref[:, c0:c0 + LANES]
            outs = []
            for hf in range(2):
                qm = jnp.where(first if hf == 0 else ~first, qp, jnp.zeros_like(qp))
                s = jnp.where(mask, _dot_nt(qm, k2), -jnp.inf)
                sink = sink_ref[j * SWA_GROUP + 2 * gp + hf]
                m = jnp.maximum(jnp.max(s, axis=-1, keepdims=True), sink)
                p = jnp.exp(s - m)
                denom = jnp.sum(p, axis=-1, keepdims=True) + jnp.exp(sink - m)
                outs.append(_dot((p / denom).astype(BF16), v2))
            o_ref[:, c0:c0 + LANES] = jnp.where(first, outs[0], outs[1]).astype(BF16)


def _swa_attn(sinks, q, k2, v2):
    B, S, nq = q.shape
    nk = k2.shape[2]
    blk = SWA_BLK
    cur = lambda n_: pl.BlockSpec((None, blk, n_), lambda b, n: (b, n, 0))
    prev = lambda n_: pl.BlockSpec((None, blk, n_), lambda b, n: (b, jnp.maximum(n - 1, 0), 0))
    return pl.pallas_call(
        _swa_attn_kernel,
        grid=(B, S // blk),
        in_specs=[pl.BlockSpec(memory_space=pltpu.SMEM), cur(nq), cur(nk), prev(nk), cur(nk), prev(nk)],
        out_specs=cur(nq),
        out_shape=jax.ShapeDtypeStruct((B, S, nq), BF16),
        compiler_params=_cparams(2),
        name="swa_attn",
    )(sinks, q, k2, k2, v2, v2)


def _top16_rows(a, nrows):
    rid = lax.broadcasted_iota(jnp.int32, a.shape, 0)
    vals, ids = [], []
    for _ in range(PEER_TOPK):
        m = jnp.max(a, axis=0, keepdims=True)
        am = jnp.min(jnp.where(a == m, rid, nrows), axis=0, keepdims=True)
        vals.append(m)
        ids.append(am)
        a = jnp.where(rid == am, -jnp.inf, a)
    return jnp.concatenate(vals, axis=0), jnp.concatenate(ids, axis=0)


def _staircase_blocks():
    K = PEER_TOPK
    blocks = []
    for i in range(4):
        nj = K // (i + 1)
        for j0 in range(0, nj, 8):
            blocks.append(("j", i, j0, [j0 + r < nj for r in range(8)]))
    for j in range(3):
        for i0 in (0, 8):
            keep = [(i0 + r >= 4) and ((i0 + r + 1) * (j + 1) <= K) for r in range(8)]
            if any(keep):
                blocks.append(("i", j, i0, keep))
    return blocks


_STAIR = _staircase_blocks()


def _peer_select_kernel(h_ref, gain_ref, wq_ref, kbd_ref, xn_ref, idx_ref, gate_ref):
    K = PEER_TOPK
    tm = h_ref.shape[0]
    xn = _rms(h_ref[...], gain_ref[...])
    xn_ref[...] = xn
    qv = _dot(xn.astype(BF16), wq_ref[...]).astype(BF16)
    st = _dot_nt(kbd_ref[...], qv)
    sub = lax.broadcasted_iota(jnp.int32, (8, tm), 0)
    experts = []
    for h in range(PEER_HEADS):
        r0 = 2 * h * PEER_KEYS
        v1, i1 = _top16_rows(st[r0:r0 + PEER_KEYS], PEER_KEYS)
        v2, i2 = _top16_rows(st[r0 + PEER_KEYS:r0 + 2 * PEER_KEYS], PEER_KEYS)
        cv, cc = [], []
        for kind, fixed, start, keep in _STAIR:
            keepm = functools.reduce(jnp.logical_or, [sub == r for r in range(8) if keep[r]])
            if kind == "j":
                val = v1[fixed:fixed + 1] + v2[start:start + 8]
                code = fixed * K + start + sub
            else:
                val = v1[start:start + 8] + v2[fixed:fixed + 1]
                code = (start + sub) * K + fixed
            cv.append(jnp.where(keepm, val, -jnp.inf))
            cc.append(code)
        cand = jnp.concatenate(cv, axis=0)
        code = jnp.concatenate(cc, axis=0)
        scs, cis = [], []
        for _ in range(K):
            m = jnp.max(cand, axis=0, keepdims=True)
            cm = jnp.min(jnp.where(cand == m, code, K * K), axis=0, keepdims=True)
            scs.append(m)
            cis.append(cm)
            cand = jnp.where(code == cm, -jnp.inf, cand)
        sc = jnp.concatenate(scs, axis=0)
        ci = jnp.concatenate(cis, axis=0)
        chi = ci >> 4
        clo = ci & (K - 1)
        e1 = jnp.zeros_like(ci)
        e2 = jnp.zeros_like(ci)
        for i in range(K):
            e1 = jnp.where(chi == i, i1[i:i + 1], e1)
            e2 = jnp.where(clo == i, i2[i:i + 1], e2)
        experts.append((e1 * PEER_KEYS + e2) * TABLE_ROWS)
        ex = jnp.exp(sc - sc[0:1])
        gate_ref[h * K:(h + 1) * K, :] = ex / jnp.sum(ex, axis=0, keepdims=True)
    idx_ref[...] = jnp.concatenate(experts, axis=0).T


def _peer_select(h2, gain, wq, kbd):
    T = h2.shape[0]
    tm = TOKEN_TILE
    full = lambda a: pl.BlockSpec(a.shape, lambda i: (0,) * a.ndim)
    return pl.pallas_call(
        _peer_select_kernel,
        grid=(T // tm,),
        in_specs=[pl.BlockSpec((tm, D_MODEL), lambda i: (i, 0)), full(gain), full(wq), full(kbd)],
        out_specs=[pl.BlockSpec((tm, D_MODEL), lambda i: (i, 0)),
                   pl.BlockSpec((tm, PEER_SLOTS), lambda i: (i, 0)),
                   pl.BlockSpec((PEER_SLOTS, tm), lambda i: (0, i))],
        out_shape=[jax.ShapeDtypeStruct((T, D_MODEL), F32),
                   jax.ShapeDtypeStruct((T, PEER_SLOTS), jnp.int32),
                   jax.ShapeDtypeStruct((PEER_SLOTS, T), F32)],
        compiler_params=_cparams(),
        name="peer_select",
    )(h2, gain, wq, kbd)


def _pack_table(w):
    wb = w.astype(BF16)
    half = D_MODEL // 2
    lo = lax.bitcast_convert_type(wb[:, :half], jnp.uint16).astype(jnp.uint32)
    hi = lax.bitcast_convert_type(wb[:, half:], jnp.uint16).astype(jnp.uint32)
    return (lo | (hi << 16)).reshape(w.shape[0], 4, LANES)


def _unpack(w):
    lo = lax.bitcast_convert_type(w << 16, F32)
    hi = lax.bitcast_convert_type(w & jnp.uint32(0xFFFF0000), F32)
    return lo, hi


def _pack_pair(lo, hi):
    return pltpu.bitcast(pltpu.pack_elementwise([lo, hi], packed_dtype=BF16), BF16)


def _pair_products(words, xb):
    prod = pltpu.bitcast(pltpu.bitcast(words, BF16) * xb, jnp.uint32)
    plo = pltpu.unpack_elementwise(prod, index=0, packed_dtype=BF16, unpacked_dtype=F32)
    phi = pltpu.unpack_elementwise(prod, index=1, packed_dtype=BF16, unpacked_dtype=F32)
    return plo, phi


def _expert_rows(tab_ref, row):
    return tab_ref[pl.ds(pl.multiple_of(row, TABLE_ROWS), TABLE_ROWS), :]


def _stream_index_groups(idx_ref, ibufs, sems, process):
    ng = PEER_TOKENS // IDX_GROUP

    def copy(g, slot):
        rows = pl.ds(pl.multiple_of(g * IDX_GROUP, IDX_GROUP), IDX_GROUP)
        return pltpu.make_async_copy(idx_ref.at[rows], ibufs[slot], sems.at[slot])

    copy(0, 0).start()

    def pair(p, carry):
        g = 2 * p
        copy(g + 1, 1).start()
        copy(g, 0).wait()
        process(g, ibufs[0])
        nxt = jnp.minimum(g + 2, ng - 1)
        copy(nxt, 0).start()
        copy(g + 1, 1).wait()
        process(g + 1, ibufs[1])
        return carry

    lax.fori_loop(0, ng // 2, pair, 0)
    copy(ng - 1, 0).wait()


def _index_scratch():
    return [pltpu.SMEM((IDX_GROUP, PEER_SLOTS), jnp.int32), pltpu.SMEM((IDX_GROUP, PEER_SLOTS), jnp.int32),
            pltpu.SemaphoreType.DMA((2,))]


def _peer_u_kernel(idx_ref, xlo_ref, xhi_ref, gate_ref, tab_ref, a_ref,
                   s0_ref, s1_ref, ib0_ref, ib1_ref, sems):
    tb = PEER_TOKENS
    ss = FOLD_STRIDE
    lane = lax.broadcasted_iota(jnp.int32, (8, tb), 1)
    folds = (s0_ref, s1_ref)

    def gather(t, j, ibuf, s_ref):
        xb = _pack_pair(xlo_ref[t], xhi_ref[t])
        for k in range(PEER_SLOTS):
            plo, phi = _pair_products(_expert_rows(tab_ref, ibuf[j, k]), xb)
            s_ref[pl.ds(k, 4, stride=ss), :] = plo + phi

    def fold(t, s_ref):
        sel = lane == t
        for j in range(PEER_SLOTS // 8):
            r = s_ref[pl.ds(8 * j, 8), :]
            for s in range(1, 4):
                r = r + s_ref[pl.ds(s * ss + 8 * j, 8), :]
            tot = jnp.sum(r, axis=1, keepdims=True)
            a_ref[pl.ds(8 * j, 8), :] = jnp.where(sel, tot, a_ref[pl.ds(8 * j, 8), :])

    def process(g, ibuf):
        for j in range(IDX_GROUP):
            t = g * IDX_GROUP + j
            gather(t, j, ibuf, folds[j % 2])
            fold(t - 1, folds[(j + 1) % 2])

    a_ref[...] = jnp.zeros_like(a_ref)
    s1_ref[...] = jnp.zeros_like(s1_ref)
    _stream_index_groups(idx_ref, (ib0_ref, ib1_ref), sems, process)
    fold(tb - 1, folds[(tb - 1) % 2])
    z = a_ref[...]
    a_ref[...] = 0.5 * z * (1.0 + lax.erf(z * (2.0 ** -0.5))) * gate_ref[...]


def _peer_u(idx, xlo, xhi, gate_t, tab):
    T = idx.shape[0]
    tb = PEER_TOKENS
    ne = tab.shape[0]
    return pl.pallas_call(
        _peer_u_kernel,
        grid=(T // tb,),
        in_specs=[
            pl.BlockSpec((tb, PEER_SLOTS), lambda i: (i, 0)),
            pl.BlockSpec((tb, 4, LANES), lambda i: (i, 0, 0)),
            pl.BlockSpec((tb, 4, LANES), lambda i: (i, 0, 0)),
            pl.BlockSpec((PEER_SLOTS, tb), lambda i: (0, i)),
            pl.BlockSpec((ne * 4, LANES), lambda i: (0, 0), pipeline_mode=pl.Buffered(1)),
        ],
        out_specs=pl.BlockSpec((PEER_SLOTS, tb), lambda i: (0, i)),
        out_shape=jax.ShapeDtypeStruct((PEER_SLOTS, T), F32),
        scratch_shapes=[pltpu.VMEM((4 * FOLD_STRIDE, LANES), F32),
                        pltpu.VMEM((4 * FOLD_STRIDE, LANES), F32)] + _index_scratch(),
        compiler_params=_cparams(),
        name="peer_u",
    )(idx, xlo, xhi, gate_t, tab.reshape(ne * 4, LANES))


def _peer_v_kernel(idx_ref, a_ref, rlo_ref, rhi_ref, tab_ref, olo_ref, ohi_ref,
                   b0_ref, b1_ref, ib0_ref, ib1_ref, sems):
    tb = PEER_TOKENS
    lane = lax.broadcasted_iota(jnp.int32, (PEER_SLOTS, tb), 1)
    bcast = (b0_ref, b1_ref)

    def prep(t, b_ref):
        col = jnp.sum(jnp.where(lane == t, a_ref[...], 0.0), axis=1, keepdims=True)
        wide = jnp.broadcast_to(col, (PEER_SLOTS, LANES))
        b_ref[...] = pltpu.pack_elementwise([wide, wide], packed_dtype=BF16)

    def accum(t, j, ibuf, b_ref):
        zero = jnp.zeros((TABLE_ROWS, LANES), F32)
        acc_lo = [rlo_ref[t]] + [zero] * (V_CHAINS - 1)
        acc_hi = [rhi_ref[t]] + [zero] * (V_CHAINS - 1)
        for k in range(PEER_SLOTS):
            ab = pltpu.bitcast(jnp.broadcast_to(b_ref[pl.ds(k, 1), :], (TABLE_ROWS, LANES)), BF16)
            plo, phi = _pair_products(_expert_rows(tab_ref, ibuf[j, k]), ab)
            acc_lo[k % V_CHAINS] = acc_lo[k % V_CHAINS] + plo
            acc_hi[k % V_CHAINS] = acc_hi[k % V_CHAINS] + phi
        olo_ref[t] = functools.reduce(jnp.add, acc_lo)
        ohi_ref[t] = functools.reduce(jnp.add, acc_hi)

    def process(g, ibuf):
        for j in range(IDX_GROUP):
            t = g * IDX_GROUP + j
            prep(jnp.minimum(t + 1, tb - 1), bcast[(j + 1) % 2])
            accum(t, j, ibuf, bcast[j % 2])

    prep(0, b0_ref)
    _stream_index_groups(idx_ref, (ib0_ref, ib1_ref), sems, process)


def _peer_v(idx, a_t, rlo, rhi, tab):
    T = idx.shape[0]
    tb = PEER_TOKENS
    ne = tab.shape[0]
    tok = pl.BlockSpec((tb, 4, LANES), lambda i: (i, 0, 0))
    return pl.pallas_call(
        _peer_v_kernel,
        grid=(T // tb,),
        in_specs=[
            pl.BlockSpec((tb, PEER_SLOTS), lambda i: (i, 0)),
            pl.BlockSpec((PEER_SLOTS, tb), lambda i: (0, i)),
            tok, tok,
            pl.BlockSpec((ne * 4, LANES), lambda i: (0, 0), pipeline_mode=pl.Buffered(1)),
        ],
        out_specs=[tok, tok],
        out_shape=[jax.ShapeDtypeStruct((T, 4, LANES), F32), jax.ShapeDtypeStruct((T, 4, LANES), F32)],
        scratch_shapes=[pltpu.VMEM((PEER_SLOTS, LANES), jnp.uint32), pltpu.VMEM((PEER_SLOTS, LANES), jnp.uint32)]
        + _index_scratch(),
        compiler_params=_cparams(),
        name="peer_v",
    )(idx, a_t, rlo, rhi, tab.reshape(ne * 4, LANES))


def _halves(a2):
    a4 = a2.reshape(a2.shape[0], 2, 4, LANES)
    return a4[:, 0], a4[:, 1]


def _peer_layer(h2, gain, wq, keys, u, v):
    T = h2.shape[0]
    eye = jnp.eye(PEER_HEADS * 2, dtype=F32)
    kbd = jnp.einsum("pkd,pq->pkqd", keys.reshape(PEER_HEADS * 2, PEER_KEYS, PEER_HALF), eye)
    kbd = kbd.reshape(PEER_HEADS * 2 * PEER_KEYS, D_MODEL).astype(BF16)
    xn, idx, gate_t = _peer_select(h2, gain.reshape(1, D_MODEL), wq.astype(BF16), kbd)
    xlo, xhi = _halves(xn)
    a_t = _peer_u(idx, xlo, xhi, gate_t, _pack_table(u))
    rlo, rhi = _halves(h2)
    olo, ohi = _peer_v(idx, a_t, rlo, rhi, _pack_table(v))
    return jnp.concatenate([olo.reshape(T, D_MODEL // 2), ohi.reshape(T, D_MODEL // 2)], axis=1)


def _gla_layer(x2, B, S, gain, w_in, w_alpha, b_alpha, norm_gain, w_out):
    hk = GLA_HEADS * GLA_DK
    hv = GLA_HEADS * GLA_DV
    w_main = w_in[:, :2 * hk + 2 * hv].astype(BF16)
    w_lr = jnp.pad(w_in[:, 2 * hk + 2 * hv:], ((0, 0), (0, LANES - GLA_RANK))).astype(BF16)
    wa = jnp.pad(w_alpha, ((0, LANES - GLA_RANK), (0, 0))).astype(BF16)
    q, k, v, r, g = _gla_in(x2, gain.reshape(1, D_MODEL), w_main, w_lr, wa, b_alpha.reshape(1, hk))
    sh = lambda a: a.reshape(B, S, a.shape[1])
    og = _gla_scan(sh(q), sh(k), sh(g), sh(v), sh(r), norm_gain.reshape(1, hv))
    return _proj_res(og.reshape(B * S, hv), w_out.astype(BF16), None, x2, "gla_out")


def _swa_layer(h2, pos, B, S, gain, w_in, b_in, q_gain, k_gain, sinks, w_out, b_out):
    nq = SWA_QH * SWA_HD
    hd = SWA_HD
    wq_, wk_, wv_ = w_in[:, :nq], w_in[:, nq:nq + SWA_KVH * hd], w_in[:, nq + SWA_KVH * hd:]
    bq_, bk_, bv_ = b_in[:nq], b_in[nq:nq + SWA_KVH * hd], b_in[nq + SWA_KVH * hd:]

    def dup(a):
        parts = []
        for j in range(SWA_KVH):
            parts += [a[..., j * hd:(j + 1) * hd]] * 2
        return jnp.concatenate(parts, axis=-1)

    w = jnp.concatenate([wq_, dup(wk_), dup(wv_)], axis=1).astype(BF16)
    b = jnp.concatenate([bq_, dup(bk_), dup(bv_)]).reshape(1, -1)
    nk = 2 * SWA_KVH * hd
    pmat = lambda n: (jnp.kron(jnp.eye(n // hd, dtype=F32), jnp.ones((hd, hd), F32)) / hd).astype(BF16)
    inv_freq = ROPE_THETA ** (-jnp.arange(0, ROPE_DIM, 2, dtype=F32) / ROPE_DIM)
    hl = np.arange(LANES) % hd
    freq = jnp.where(hl < ROPE_DIM, inv_freq[hl % (ROPE_DIM // 2)], 0.0).reshape(1, LANES)
    q, k2, v2 = _swa_in(h2, pos.reshape(B * S, 1), gain.reshape(1, D_MODEL), w, b, pmat(nq), pmat(nk),
                        jnp.tile(q_gain, nq // hd).reshape(1, nq), jnp.tile(k_gain, nk // hd).reshape(1, nk), freq)
    o = _swa_attn(sinks, q.reshape(B, S, nq), k2.reshape(B, S, nk), v2.reshape(B, S, nk))
    return _proj_res(o.reshape(B * S, nq), w_out.astype(BF16), b_out.reshape(1, D_MODEL), h2, "swa_out")


def kernel(x, positions, ln_mix, ln_ffn, gla_w_in, gla_w_alpha, gla_b_alpha, gla_norm, gla_w_out, swa_w_in, swa_b_in, swa_q_norm, swa_k_norm, swa_sinks, swa_w_out, swa_b_out, peer_w_q, peer_keys, peer_u, peer_v):
    B, S, D = x.shape
    h = x.reshape(B * S, D)
    h = _gla_layer(h, B, S, ln_mix[0], gla_w_in[0], gla_w_alpha[0], gla_b_alpha[0], gla_norm[0], gla_w_out[0])
    h = _peer_layer(h, ln_ffn[0], peer_w_q[0], peer_keys[0], peer_u[0], peer_v[0])
    h = _swa_layer(h, positions, B, S, ln_mix[1], swa_w_in[0], swa_b_in[0], swa_q_norm[0], swa_k_norm[0],
                   swa_sinks[0], swa_w_out[0], swa_b_out[0])
    h = _peer_layer(h, ln_ffn[1], peer_w_q[1], peer_keys[1], peer_u[1], peer_v[1])
    return h.reshape(B, S, D)
```

```python
import functools

import jax
import jax.numpy as jnp
import numpy as np
from jax import lax
from jax.experimental import pallas as pl
from jax.experimental.pallas import tpu as pltpu

F32 = jnp.float32
BF16 = jnp.bfloat16

D_MODEL = 1024
RMS_EPS = 1e-6

GLA_HEADS = 4
GLA_DK = 128
GLA_DV = 256
GLA_RANK = 16
GLA_TAU = 16.0
GLA_CHUNK = 64

SWA_HD = 64
SWA_QH = 16
SWA_KVH = 2
SWA_GROUP = 8
SWA_BLK = 128
ROPE_THETA = 500000.0
ROPE_DIM = 16

PEER_HEADS = 8
PEER_KEYS = 128
PEER_EXPERTS = PEER_KEYS * PEER_KEYS
PEER_HALF = 64
PEER_TOPK = 16
PEER_SLOTS = PEER_HEADS * PEER_TOPK

LANES = 128
TOKEN_TILE = 256
PEER_TOKENS = 128
IDX_GROUP = 16
TABLE_ROWS = 4
V_CHAINS = 4
FOLD_STRIDE = 136
VMEM_LIMIT = 48 * 1024 * 1024


def _cparams(n_axes=1):
    return pltpu.CompilerParams(
        dimension_semantics=("arbitrary",) * n_axes, vmem_limit_bytes=VMEM_LIMIT)


def _rms(x, gain):
    ms = jnp.mean(x * x, axis=-1, keepdims=True)
    return x * lax.rsqrt(ms + RMS_EPS) * gain


def _dot(a, b):
    return jnp.dot(a, b, preferred_element_type=F32)


def _dot_nt(a, b):
    return lax.dot_general(a, b, (((1,), (1,)), ((), ())), preferred_element_type=F32)


def _dot_tn(a, b):
    return lax.dot_general(a, b, (((0,), (0,)), ((), ())), preferred_element_type=F32)


def _split2(x):
    hi = x.astype(BF16)
    lo = (x - hi.astype(F32)).astype(BF16)
    return hi, lo


def _split3(x):
    a = x.astype(BF16)
    r = x - a.astype(F32)
    b = r.astype(BF16)
    c = (r - b.astype(F32)).astype(BF16)
    return a, b, c


def _gla_in_kernel(x_ref, gain_ref, w_ref, wlr_ref, wa_ref, ba_ref,
                   q_ref, k_ref, v_ref, r_ref, g_ref):
    hk = GLA_HEADS * GLA_DK
    hv = GLA_HEADS * GLA_DV
    xn = _rms(x_ref[...], gain_ref[...]).astype(BF16)
    q_ref[...] = _dot(xn, w_ref[:, 0:hk])
    k_ref[...] = _dot(xn, w_ref[:, hk:2 * hk])
    v_ref[...] = _dot(xn, w_ref[:, 2 * hk:2 * hk + hv]).astype(BF16)
    r_ref[...] = _dot(xn, w_ref[:, 2 * hk + hv:2 * hk + 2 * hv])
    lr = _dot(xn, wlr_ref[...])
    z = _dot(lr.astype(BF16), wa_ref[...]) + ba_ref[...]
    g_ref[...] = (jnp.minimum(z, 0.0) - jnp.log(1.0 + jnp.exp(-jnp.abs(z)))) / GLA_TAU


def _gla_in(x2, gain, w_main, w_lr, w_alpha, b_alpha):
    T = x2.shape[0]
    hk = GLA_HEADS * GLA_DK
    hv = GLA_HEADS * GLA_DV
    tm = TOKEN_TILE
    row = lambda n: pl.BlockSpec((tm, n), lambda i: (i, 0))
    full = lambda a: pl.BlockSpec(a.shape, lambda i: (0,) * a.ndim)
    return pl.pallas_call(
        _gla_in_kernel,
        grid=(T // tm,),
        in_specs=[row(D_MODEL), full(gain), full(w_main), full(w_lr), full(w_alpha), full(b_alpha)],
        out_specs=[row(hk), row(hk), row(hv), row(hv), row(hk)],
        out_shape=[jax.ShapeDtypeStruct((T, hk), F32), jax.ShapeDtypeStruct((T, hk), F32),
                   jax.ShapeDtypeStruct((T, hv), BF16), jax.ShapeDtypeStruct((T, hv), F32),
                   jax.ShapeDtypeStruct((T, hk), F32)],
        compiler_params=_cparams(),
        name="gla_in",
    )(x2, gain, w_main, w_lr, w_alpha, b_alpha)


def _gla_scan_kernel(q_ref, k_ref, g_ref, v_ref, r_ref, gain_ref, o_ref, st_ref):
    C = GLA_CHUNK
    nb = q_ref.shape[0]

    @pl.when(pl.program_id(0) == 0)
    def _():
        st_ref[...] = jnp.zeros_like(st_ref)

    ri = lax.broadcasted_iota(jnp.int32, (C, C), 0)
    ci = lax.broadcasted_iota(jnp.int32, (C, C), 1)
    tri = ri >= ci
    tri_b = jnp.where(tri, 1.0, 0.0).astype(BF16)
    for b in range(nb):
        g1, g2, g3 = _split3(g_ref[b])
        bc = _dot(tri_b, g1) + _dot(tri_b, g2) + _dot(tri_b, g3)
        eb = jnp.exp(bc)
        enb = jnp.exp(-bc)
        ebl = jnp.exp(bc[C - 1:C, :])
        qd = q_ref[b] * (GLA_DK ** -0.5) * eb
        ki = k_ref[b] * enb
        kf = ki * ebl
        for h in range(GLA_HEADS):
            ks = slice(h * GLA_DK, (h + 1) * GLA_DK)
            vs = slice(h * GLA_DV, (h + 1) * GLA_DV)
            qd_h = qd[:, ks].astype(BF16)
            ki_h = ki[:, ks].astype(BF16)
            kf_h = kf[:, ks].astype(BF16)
            v_h = v_ref[b, :, vs]
            att = jnp.where(tri, _dot_nt(qd_h, ki_h), 0.0).astype(BF16)
            st = st_ref[b * GLA_HEADS + h]
            o = _dot(att, v_h) + _dot_nt(qd_h, st.astype(BF16))
            st_ref[b * GLA_HEADS + h] = st * ebl[:, ks] + _dot_tn(v_h, kf_h)
            ms = jnp.mean(o * o, axis=-1, keepdims=True)
            on = o * lax.rsqrt(ms + RMS_EPS) * gain_ref[:, vs]
            rr = r_ref[b, :, vs]
            o_ref[b, :, vs] = (on * (rr / (1.0 + jnp.exp(-rr)))).astype(BF16)


def _gla_scan(q, k, g, v, r, gain):
    B, S, hk = q.shape
    hv = v.shape[2]
    C = GLA_CHUNK
    blk = lambda n: pl.BlockSpec((B, C, n), lambda c: (0, c, 0))
    return pl.pallas_call(
        _gla_scan_kernel,
        grid=(S // C,),
        in_specs=[blk(hk), blk(hk), blk(hk), blk(hv), blk(hv),
                  pl.BlockSpec(gain.shape, lambda c: (0, 0))],
        out_specs=blk(hv),
        out_shape=jax.ShapeDtypeStruct((B, S, hv), BF16),
        scratch_shapes=[pltpu.VMEM((B * GLA_HEADS, GLA_DV, GLA_DK), F32)],
        compiler_params=_cparams(),
        name="gla_scan",
    )(q, k, g, v, r, gain)


def _proj_res_kernel(a_ref, w_ref, res_ref, o_ref):
    o_ref[...] = res_ref[...] + _dot(a_ref[...], w_ref[...])


def _proj_bias_res_kernel(a_ref, w_ref, b_ref, res_ref, o_ref):
    o_ref[...] = res_ref[...] + (_dot(a_ref[...], w_ref[...]) + b_ref[...])


def _proj_res(a, w, bias, res, name):
    T, K = a.shape
    N = w.shape[1]
    tm = TOKEN_TILE
    row = lambda n: pl.BlockSpec((tm, n), lambda i: (i, 0))
    wspec = pl.BlockSpec((K, N), lambda i: (0, 0))
    if bias is None:
        body, specs, args = _proj_res_kernel, [row(K), wspec, row(N)], (a, w, res)
    else:
        body = _proj_bias_res_kernel
        specs = [row(K), wspec, pl.BlockSpec((1, N), lambda i: (0, 0)), row(N)]
        args = (a, w, bias, res)
    return pl.pallas_call(
        body,
        grid=(T // tm,),
        in_specs=specs,
        out_specs=row(N),
        out_shape=jax.ShapeDtypeStruct((T, N), F32),
        compiler_params=_cparams(),
        name=name,
    )(*args)


def _swa_in_kernel(h_ref, pos_ref, gain_ref, w_ref, b_ref, pq_ref, pk_ref, qg_ref, kg_ref,
                   freq_ref, q_ref, k_ref, v_ref):
    nq = SWA_QH * SWA_HD
    nk = 2 * SWA_KVH * SWA_HD
    xn = _rms(h_ref[...], gain_ref[...]).astype(BF16)
    proj = _dot(xn, w_ref[...]) + b_ref[...]
    q = proj[:, 0:nq]
    k = proj[:, nq:nq + nk]
    v = proj[:, nq + nk:nq + 2 * nk]

    def headnorm(t, p_ref, gain):
        s1, s2 = _split2(t * t)
        ms = _dot(s1, p_ref[...]) + _dot(s2, p_ref[...])
        return t * lax.rsqrt(ms + RMS_EPS) * gain

    q = headnorm(q, pq_ref, qg_ref[...])
    k = headnorm(k, pk_ref, kg_ref[...])

    ang = pos_ref[...].astype(F32) * freq_ref[...]
    cs = jnp.cos(ang)
    sn = jnp.sin(ang)
    hl = lax.broadcasted_iota(jnp.int32, ang.shape, 1) & (SWA_HD - 1)
    half = ROPE_DIM // 2
    s_lo = jnp.where(hl < half, -sn, 0.0)
    s_hi = jnp.where((hl >= half) & (hl < ROPE_DIM), sn, 0.0)

    def rope(t):
        outs = []
        for c in range(t.shape[1] // LANES):
            x = t[:, c * LANES:(c + 1) * LANES]
            x_up = pltpu.roll(x, LANES - half, axis=1)
            x_dn = pltpu.roll(x, half, axis=1)
            outs.append(x * cs + x_up * s_lo + x_dn * s_hi)
        return jnp.concatenate(outs, axis=1)

    q_ref[...] = (rope(q) * (SWA_HD ** -0.5)).astype(BF16)
    k_ref[...] = rope(k).astype(BF16)
    v_ref[...] = v.astype(BF16)


def _swa_in(h2, pos, gain, w, b, pq, pk, qg, kg, freq):
    T = h2.shape[0]
    nq = SWA_QH * SWA_HD
    nk = 2 * SWA_KVH * SWA_HD
    tm = TOKEN_TILE
    row = lambda n: pl.BlockSpec((tm, n), lambda i: (i, 0))
    full = lambda a: pl.BlockSpec(a.shape, lambda i: (0,) * a.ndim)
    return pl.pallas_call(
        _swa_in_kernel,
        grid=(T // tm,),
        in_specs=[row(D_MODEL), row(1), full(gain), full(w), full(b), full(pq), full(pk),
                  full(qg), full(kg), full(freq)],
        out_specs=[row(nq), row(nk), row(nk)],
        out_shape=[jax.ShapeDtypeStruct((T, nq), BF16), jax.ShapeDtypeStruct((T, nk), BF16),
                   jax.ShapeDtypeStruct((T, nk), BF16)],
        compiler_params=_cparams(),
        name="swa_in",
    )(h2, pos, gain, w, b, pq, pk, qg, kg, freq)


def _swa_attn_kernel(sink_ref, q_ref, kc_ref, kp_ref, vc_ref, vp_ref, o_ref):
    n = pl.program_id(1)
    blk = SWA_BLK
    kk = jnp.concatenate([kp_ref[...], kc_ref[...]], axis=0)
    vv = jnp.concatenate([vp_ref[...], vc_ref[...]], axis=0)
    qi = lax.broadcasted_iota(jnp.int32, (blk, 2 * blk), 0)
    ki = lax.broadcasted_iota(jnp.int32, (blk, 2 * blk), 1)
    rel = qi + blk - ki
    first_key = jnp.where(n > 0, 0, blk)
    mask = (rel >= 0) & (rel < blk) & (ki >= first_key)
    lane = lax.broadcasted_iota(jnp.int32, (blk, LANES), 1)
    first = lane < SWA_HD
    for j in range(SWA_KVH):
        k2 = kk[:, j * LANES:(j + 1) * LANES]
        v2 = vv[:, j * LANES:(j + 1) * LANES]
        for gp in range(SWA_GROUP // 2):
            c0 = j * (SWA_GROUP * SWA_HD) + gp * LANES
            qp = q_ref[:, c0:c0 + LANES]
            outs = []
            for hf in range(2):
                qm = jnp.where(first if hf == 0 else ~first, qp, jnp.zeros_like(qp))
                s = jnp.where(mask, _dot_nt(qm, k2), -jnp.inf)
                sink = sink_ref[j * SWA_GROUP + 2 * gp + hf]
                m = jnp.maximum(jnp.max(s, axis=-1, keepdims=True), sink)
                p = jnp.exp(s - m)
                denom = jnp.sum(p, axis=-1, keepdims=True) + jnp.exp(sink - m)
                outs.append(_dot((p / denom).astype(BF16), v2))
            o_ref[:, c0:c0 + LANES] = jnp.where(first, outs[0], outs[1]).astype(BF16)


def _swa_attn(sinks, q, k2, v2):
    B, S, nq = q.shape
    nk = k2.shape[2]
    blk = SWA_BLK
    cur = lambda n_: pl.BlockSpec((None, blk, n_), lambda b, n: (b, n, 0))
    prev = lambda n_: pl.BlockSpec((None, blk, n_), lambda b, n: (b, jnp.maximum(n - 1, 0), 0))
    return pl.pallas_call(
        _swa_attn_kernel,
        grid=(B, S // blk),
        in_specs=[pl.BlockSpec(memory_space=pltpu.SMEM), cur(nq), cur(nk), prev(nk), cur(nk), prev(nk)],
        out_specs=cur(nq),
        out_shape=jax.ShapeDtypeStruct((B, S, nq), BF16),
        compiler_params=_cparams(2),
        name="swa_attn",
    )(sinks, q, k2, k2, v2, v2)


def _top16_rows(a, nrows):
    rid = lax.broadcasted_iota(jnp.int32, a.shape, 0)
    vals, ids = [], []
    for _ in range(PEER_TOPK):
        m = jnp.max(a, axis=0, keepdims=True)
        am = jnp.min(jnp.where(a == m, rid, nrows), axis=0, keepdims=True)
        vals.append(m)
        ids.append(am)
        a = jnp.where(rid == am, -jnp.inf, a)
    return jnp.concatenate(vals, axis=0), jnp.concatenate(ids, axis=0)


def _staircase_blocks():
    K = PEER_TOPK
    blocks = []
    for i in range(4):
        nj = K // (i + 1)
        for j0 in range(0, nj, 8):
            blocks.append(("j", i, j0, [j0 + r < nj for r in range(8)]))
    for j in range(3):
        for i0 in (0, 8):
            keep = [(i0 + r >= 4) and ((i0 + r + 1) * (j + 1) <= K) for r in range(8)]
            if any(keep):
                blocks.append(("i", j, i0, keep))
    return blocks


_STAIR = _staircase_blocks()


def _peer_select_kernel(h_ref, gain_ref, wq_ref, kbd_ref, xn_ref, idx_ref, gate_ref):
    K = PEER_TOPK
    tm = h_ref.shape[0]
    xn = _rms(h_ref[...], gain_ref[...])
    xn_ref[...] = xn
    qv = _dot(xn.astype(BF16), wq_ref[...]).astype(BF16)
    st = _dot_nt(kbd_ref[...], qv)
    sub = lax.broadcasted_iota(jnp.int32, (8, tm), 0)
    experts = []
    for h in range(PEER_HEADS):
        r0 = 2 * h * PEER_KEYS
        v1, i1 = _top16_rows(st[r0:r0 + PEER_KEYS], PEER_KEYS)
        v2, i2 = _top16_rows(st[r0 + PEER_KEYS:r0 + 2 * PEER_KEYS], PEER_KEYS)
        cv, cc = [], []
        for kind, fixed, start, keep in _STAIR:
            keepm = functools.reduce(jnp.logical_or, [sub == r for r in range(8) if keep[r]])
            if kind == "j":
                val = v1[fixed:fixed + 1] + v2[start:start + 8]
                code = fixed * K + start + sub
            else:
                val = v1[start:start + 8] + v2[fixed:fixed + 1]
                code = (start + sub) * K + fixed
            cv.append(jnp.where(keepm, val, -jnp.inf))
            cc.append(code)
        cand = jnp.concatenate(cv, axis=0)
        code = jnp.concatenate(cc, axis=0)
        scs, cis = [], []
        for _ in range(K):
            m = jnp.max(cand, axis=0, keepdims=True)
            cm = jnp.min(jnp.where(cand == m, code, K * K), axis=0, keepdims=True)
            scs.append(m)
            cis.append(cm)
            cand = jnp.where(code == cm, -jnp.inf, cand)
        sc = jnp.concatenate(scs, axis=0)
        ci = jnp.concatenate(cis, axis=0)
        chi = ci >> 4
        clo = ci & (K - 1)
        e1 = jnp.zeros_like(ci)
        e2 = jnp.zeros_like(ci)
        for i in range(K):
            e1 = jnp.where(chi == i, i1[i:i + 1], e1)
            e2 = jnp.where(clo == i, i2[i:i + 1], e2)
        experts.append((e1 * PEER_KEYS + e2) * TABLE_ROWS)
        ex = jnp.exp(sc - sc[0:1])
        gate_ref[h * K:(h + 1) * K, :] = ex / jnp.sum(ex, axis=0, keepdims=True)
    idx_ref[...] = jnp.concatenate(experts, axis=0).T


def _peer_select(h2, gain, wq, kbd):
    T = h2.shape[0]
    tm = TOKEN_TILE
    full = lambda a: pl.BlockSpec(a.shape, lambda i: (0,) * a.ndim)
    return pl.pallas_call(
        _peer_select_kernel,
        grid=(T // tm,),
        in_specs=[pl.BlockSpec((tm, D_MODEL), lambda i: (i, 0)), full(gain), full(wq), full(kbd)],
        out_specs=[pl.BlockSpec((tm, D_MODEL), lambda i: (i, 0)),
                   pl.BlockSpec((tm, PEER_SLOTS), lambda i: (i, 0)),
                   pl.BlockSpec((PEER_SLOTS, tm), lambda i: (0, i))],
        out_shape=[jax.ShapeDtypeStruct((T, D_MODEL), F32),
                   jax.ShapeDtypeStruct((T, PEER_SLOTS), jnp.int32),
                   jax.ShapeDtypeStruct((PEER_SLOTS, T), F32)],
        compiler_params=_cparams(),
        name="peer_select",
    )(h2, gain, wq, kbd)


def _pack_table(w):
    wb = w.astype(BF16)
    half = D_MODEL // 2
    lo = lax.bitcast_convert_type(wb[:, :half], jnp.uint16).astype(jnp.uint32)
    hi = lax.bitcast_convert_type(wb[:, half:], jnp.uint16).astype(jnp.uint32)
    return (lo | (hi << 16)).reshape(w.shape[0], 4, LANES)


def _unpack(w):
    lo = lax.bitcast_convert_type(w << 16, F32)
    hi = lax.bitcast_convert_type(w & jnp.uint32(0xFFFF0000), F32)
    return lo, hi


def _pack_pair(lo, hi):
    return pltpu.bitcast(pltpu.pack_elementwise([lo, hi], packed_dtype=BF16), BF16)


def _pair_products(words, xb):
    prod = pltpu.bitcast(pltpu.bitcast(words, BF16) * xb, jnp.uint32)
    plo = pltpu.unpack_elementwise(prod, index=0, packed_dtype=BF16, unpacked_dtype=F32)
    phi = pltpu.unpack_elementwise(prod, index=1, packed_dtype=BF16, unpacked_dtype=F32)
    return plo, phi


def _expert_rows(tab_ref, row):
    return tab_ref[pl.ds(pl.multiple_of(row, TABLE_ROWS), TABLE_ROWS), :]


def _stream_index_groups(idx_ref, ibufs, sems, process):
    ng = PEER_TOKENS // IDX_GROUP

    def copy(g, slot):
        rows = pl.ds(pl.multiple_of(g * IDX_GROUP, IDX_GROUP), IDX_GROUP)
        return pltpu.make_async_copy(idx_ref.at[rows], ibufs[slot], sems.at[slot])

    copy(0, 0).start()

    def pair(p, carry):
        g = 2 * p
        copy(g + 1, 1).start()
        copy(g, 0).wait()
        process(g, ibufs[0])
        nxt = jnp.minimum(g + 2, ng - 1)
        copy(nxt, 0).start()
        copy(g + 1, 1).wait()
        process(g + 1, ibufs[1])
        return carry

    lax.fori_loop(0, ng // 2, pair, 0)
    copy(ng - 1, 0).wait()


def _index_scratch():
    return [pltpu.SMEM((IDX_GROUP, PEER_SLOTS), jnp.int32), pltpu.SMEM((IDX_GROUP, PEER_SLOTS), jnp.int32),
            pltpu.SemaphoreType.DMA((2,))]


def _peer_u_kernel(idx_ref, xlo_ref, xhi_ref, gate_ref, tab_ref, a_ref,
                   s0_ref, s1_ref, ib0_ref, ib1_ref, sems):
    tb = PEER_TOKENS
    ss = FOLD_STRIDE
    lane = lax.broadcasted_iota(jnp.int32, (8, tb), 1)
    folds = (s0_ref, s1_ref)

    def gather(t, j, ibuf, s_ref):
        xb = _pack_pair(xlo_ref[t], xhi_ref[t])
        for k in range(PEER_SLOTS):
            plo, phi = _pair_products(_expert_rows(tab_ref, ibuf[j, k]), xb)
            s_ref[pl.ds(k, 4, stride=ss), :] = plo + phi

    def fold(t, s_ref):
        sel = lane == t
        for j in range(PEER_SLOTS // 8):
            r = s_ref[pl.ds(8 * j, 8), :]
            for s in range(1, 4):
                r = r + s_ref[pl.ds(s * ss + 8 * j, 8), :]
            tot = jnp.sum(r, axis=1, keepdims=True)
            a_ref[pl.ds(8 * j, 8), :] = jnp.where(sel, tot, a_ref[pl.ds(8 * j, 8), :])

    def process(g, ibuf):
        for j in range(IDX_GROUP):
            t = g * IDX_GROUP + j
            gather(t, j, ibuf, folds[j % 2])
            fold(t - 1, folds[(j + 1) % 2])

    a_ref[...] = jnp.zeros_like(a_ref)
    s1_ref[...] = jnp.zeros_like(s1_ref)
    _stream_index_groups(idx_ref, (ib0_ref, ib1_ref), sems, process)
    fold(tb - 1, folds[(tb - 1) % 2])
    z = a_ref[...]
    a_ref[...] = 0.5 * z * (1.0 + lax.erf(z * (2.0 ** -0.5))) * gate_ref[...]


def _peer_u(idx, xlo, xhi, gate_t, tab):
    T = idx.shape[0]
    tb = PEER_TOKENS
    ne = tab.shape[0]
    return pl.pallas_call(
        _peer_u_kernel,
        grid=(T // tb,),
        in_specs=[
            pl.BlockSpec((tb, PEER_SLOTS), lambda i: (i, 0)),
            pl.BlockSpec((tb, 4, LANES), lambda i: (i, 0, 0)),
            pl.BlockSpec((tb, 4, LANES), lambda i: (i, 0, 0)),
            pl.BlockSpec((PEER_SLOTS, tb), lambda i: (0, i)),
            pl.BlockSpec((ne * 4, LANES), lambda i: (0, 0), pipeline_mode=pl.Buffered(1)),
        ],
        out_specs=pl.BlockSpec((PEER_SLOTS, tb), lambda i: (0, i)),
        out_shape=jax.ShapeDtypeStruct((PEER_SLOTS, T), F32),
        scratch_shapes=[pltpu.VMEM((4 * FOLD_STRIDE, LANES), F32),
                        pltpu.VMEM((4 * FOLD_STRIDE, LANES), F32)] + _index_scratch(),
        compiler_params=_cparams(),
        name="peer_u",
    )(idx, xlo, xhi, gate_t, tab.reshape(ne * 4, LANES))


def _peer_v_kernel(idx_ref, a_ref, rlo_ref, rhi_ref, tab_ref, olo_ref, ohi_ref,
                   b0_ref, b1_ref, ib0_ref, ib1_ref, sems):
    tb = PEER_TOKENS
    lane = lax.broadcasted_iota(jnp.int32, (PEER_SLOTS, tb), 1)
    bcast = (b0_ref, b1_ref)

    def prep(t, b_ref):
        col = jnp.sum(jnp.where(lane == t, a_ref[...], 0.0), axis=1, keepdims=True)
        wide = jnp.broadcast_to(col, (PEER_SLOTS, LANES))
        b_ref[...] = pltpu.pack_elementwise([wide, wide], packed_dtype=BF16)

    def accum(t, j, ibuf, b_ref):
        zero = jnp.zeros((TABLE_ROWS, LANES), F32)
        acc_lo = [rlo_ref[t]] + [zero] * (V_CHAINS - 1)
        acc_hi = [rhi_ref[t]] + [zero] * (V_CHAINS - 1)
        for k in range(PEER_SLOTS):
            ab = pltpu.bitcast(jnp.broadcast_to(b_ref[pl.ds(k, 1), :], (TABLE_ROWS, LANES)), BF16)
            plo, phi = _pair_products(_expert_rows(tab_ref, ibuf[j, k]), ab)
            acc_lo[k % V_CHAINS] = acc_lo[k % V_CHAINS] + plo
            acc_hi[k % V_CHAINS] = acc_hi[k % V_CHAINS] + phi
        olo_ref[t] = functools.reduce(jnp.add, acc_lo)
        ohi_ref[t] = functools.reduce(jnp.add, acc_hi)

    def process(g, ibuf):
        for j in range(IDX_GROUP):
            t = g * IDX_GROUP + j
            prep(jnp.minimum(t + 1, tb - 1), bcast[(j + 1) % 2])
            accum(t, j, ibuf, bcast[j % 2])

    prep(0, b0_ref)
    _stream_index_groups(idx_ref, (ib0_ref, ib1_ref), sems, process)


def _peer_v(idx, a_t, rlo, rhi, tab):
    T = idx.shape[0]
    tb = PEER_TOKENS
    ne = tab.shape[0]
    tok = pl.BlockSpec((tb, 4, LANES), lambda i: (i, 0, 0))
    return pl.pallas_call(
        _peer_v_kernel,
        grid=(T // tb,),
        in_specs=[
            pl.BlockSpec((tb, PEER_SLOTS), lambda i: (i, 0)),
            pl.BlockSpec((PEER_SLOTS, tb), lambda i: (0, i)),
            tok, tok,
            pl.BlockSpec((ne * 4, LANES), lambda i: (0, 0), pipeline_mode=pl.Buffered(1)),
        ],
        out_specs=[tok, tok],
        out_shape=[jax.ShapeDtypeStruct((T, 4, LANES), F32), jax.ShapeDtypeStruct((T, 4, LANES), F32)],
        scratch_shapes=[pltpu.VMEM((PEER_SLOTS, LANES), jnp.uint32), pltpu.VMEM((PEER_SLOTS, LANES), jnp.uint32)]
        + _index_scratch(),
        compiler_params=_cparams(),
        name="peer_v",
    )(idx, a_t, rlo, rhi, tab.reshape(ne * 4, LANES))


def _halves(a2):
    a4 = a2.reshape(a2.shape[0], 2, 4, LANES)
    return a4[:, 0], a4[:, 1]


def _peer_layer(h2, gain, wq, keys, u, v):
    T = h2.shape[0]
    eye = jnp.eye(PEER_HEADS * 2, dtype=F32)
    kbd = jnp.einsum("pkd,pq->pkqd", keys.reshape(PEER_HEADS * 2, PEER_KEYS, PEER_HALF), eye)
    kbd = kbd.reshape(PEER_HEADS * 2 * PEER_KEYS, D_MODEL).astype(BF16)
    xn, idx, gate_t = _peer_select(h2, gain.reshape(1, D_MODEL), wq.astype(BF16), kbd)
    xlo, xhi = _halves(xn)
    a_t = _peer_u(idx, xlo, xhi, gate_t, _pack_table(u))
    rlo, rhi = _halves(h2)
    olo, ohi = _peer_v(idx, a_t, rlo, rhi, _pack_table(v))
    return jnp.concatenate([olo.reshape(T, D_MODEL // 2), ohi.reshape(T, D_MODEL // 2)], axis=1)


def _gla_layer(x2, B, S, gain, w_in, w_alpha, b_alpha, norm_gain, w_out):
    hk = GLA_HEADS * GLA_DK
    hv = GLA_HEADS * GLA_DV
    w_main = w_in[:, :2 * hk + 2 * hv].astype(BF16)
    w_lr = jnp.pad(w_in[:, 2 * hk + 2 * hv:], ((0, 0), (0, LANES - GLA_RANK))).astype(BF16)
    wa = jnp.pad(w_alpha, ((0, LANES - GLA_RANK), (0, 0))).astype(BF16)
    q, k, v, r, g = _gla_in(x2, gain.reshape(1, D_MODEL), w_main, w_lr, wa, b_alpha.reshape(1, hk))
    sh = lambda a: a.reshape(B, S, a.shape[1])
    og = _gla_scan(sh(q), sh(k), sh(g), sh(v), sh(r), norm_gain.reshape(1, hv))
    return _proj_res(og.reshape(B * S, hv), w_out.astype(BF16), None, x2, "gla_out")


def _swa_layer(h2, pos, B, S, gain, w_in, b_in, q_gain, k_gain, sinks, w_out, b_out):
    nq = SWA_QH * SWA_HD
    hd = SWA_HD
    wq_, wk_, wv_ = w_in[:, :nq], w_in[:, nq:nq + SWA_KVH * hd], w_in[:, nq + SWA_KVH * hd:]
    bq_, bk_, bv_ = b_in[:nq], b_in[nq:nq + SWA_KVH * hd], b_in[nq + SWA_KVH * hd:]

    def dup(a):
        parts = []
        for j in range(SWA_KVH):
            parts += [a[..., j * hd:(j + 1) * hd]] * 2
        return jnp.concatenate(parts, axis=-1)

    w = jnp.concatenate([wq_, dup(wk_), dup(wv_)], axis=1).astype(BF16)
    b = jnp.concatenate([bq_, dup(bk_), dup(bv_)]).reshape(1, -1)
    nk = 2 * SWA_KVH * hd
    pmat = lambda n: (jnp.kron(jnp.eye(n // hd, dtype=F32), jnp.ones((hd, hd), F32)) / hd).astype(BF16)
    inv_freq = ROPE_THETA ** (-jnp.arange(0, ROPE_DIM, 2, dtype=F32) / ROPE_DIM)
    hl = np.arange(LANES) % hd
    freq = jnp.where(hl < ROPE_DIM, inv_freq[hl % (ROPE_DIM // 2)], 0.0).reshape(1, LANES)
    q, k2, v2 = _swa_in(h2, pos.reshape(B * S, 1), gain.reshape(1, D_MODEL), w, b, pmat(nq), pmat(nk),
                        jnp.tile(q_gain, nq // hd).reshape(1, nq), jnp.tile(k_gain, nk // hd).reshape(1, nk), freq)
    o = _swa_attn(sinks, q.reshape(B, S, nq), k2.reshape(B, S, nk), v2.reshape(B, S, nk))
    return _proj_res(o.reshape(B * S, nq), w_out.astype(BF16), b_out.reshape(1, D_MODEL), h2, "swa_out")


def kernel(x, positions, ln_mix, ln_ffn, gla_w_in, gla_w_alpha, gla_b_alpha, gla_norm, gla_w_out, swa_w_in, swa_b_in, swa_q_norm, swa_k_norm, swa_sinks, swa_w_out, swa_b_out, peer_w_q, peer_keys, peer_u, peer_v):
    B, S, D = x.shape
    h = x.reshape(B * S, D)
    h = _gla_layer(h, B, S, ln_mix[0], gla_w_in[0], gla_w_alpha[0], gla_b_alpha[0], gla_norm[0], gla_w_out[0])
    h = _peer_layer(h, ln_ffn[0], peer_w_q[0], peer_keys[0], peer_u[0], peer_v[0])
    h = _swa_layer(h, positions, B, S, ln_mix[1], swa_w_in[0], swa_b_in[0], swa_q_norm[0], swa_k_norm[0],
                   swa_sinks[0], swa_w_out[0], swa_b_out[0])
    h = _peer_layer(h, ln_ffn[1], peer_w_q[1], peer_keys[1], peer_u[1], peer_v[1])
    return h.reshape(B, S, D)
```

```python
import functools

import jax
import jax.numpy as jnp
import numpy as np
from jax import lax
from jax.experimental import pallas as pl
from jax.experimental.pallas import tpu as pltpu

F32 = jnp.float32
BF16 = jnp.bfloat16

D_MODEL = 1024
RMS_EPS = 1e-6

GLA_HEADS = 4
GLA_DK = 128
GLA_DV = 256
GLA_RANK = 16
GLA_TAU = 16.0
GLA_CHUNK = 64

SWA_HD = 64
SWA_QH = 16
SWA_KVH = 2
SWA_GROUP = 8
SWA_BLK = 128
ROPE_THETA = 500000.0
ROPE_DIM = 16

PEER_HEADS = 8
PEER_KEYS = 128
PEER_EXPERTS = PEER_KEYS * PEER_KEYS
PEER_HALF = 64
PEER_TOPK = 16
PEER_SLOTS = PEER_HEADS * PEER_TOPK

LANES = 128
TOKEN_TILE = 256
PEER_TOKENS = 128
IDX_GROUP = 32
TABLE_ROWS = 4
V_CHAINS = 4
FOLD_STRIDE = 136
VMEM_LIMIT = 48 * 1024 * 1024


def _cparams(n_axes=1):
    return pltpu.CompilerParams(
        dimension_semantics=("arbitrary",) * n_axes, vmem_limit_bytes=VMEM_LIMIT)


def _rms(x, gain):
    ms = jnp.mean(x * x, axis=-1, keepdims=True)
    return x * lax.rsqrt(ms + RMS_EPS) * gain


def _dot(a, b):
    return jnp.dot(a, b, preferred_element_type=F32)


def _dot_nt(a, b):
    return lax.dot_general(a, b, (((1,), (1,)), ((), ())), preferred_element_type=F32)


def _dot_tn(a, b):
    return lax.dot_general(a, b, (((0,), (0,)), ((), ())), preferred_element_type=F32)


def _split2(x):
    hi = x.astype(BF16)
    lo = (x - hi.astype(F32)).astype(BF16)
    return hi, lo


def _split3(x):
    a = x.astype(BF16)
    r = x - a.astype(F32)
    b = r.astype(BF16)
    c = (r - b.astype(F32)).astype(BF16)
    return a, b, c


def _gla_in_kernel(x_ref, gain_ref, w_ref, wlr_ref, wa_ref, ba_ref,
                   q_ref, k_ref, v_ref, r_ref, g_ref):
    hk = GLA_HEADS * GLA_DK
    hv = GLA_HEADS * GLA_DV
    xn = _rms(x_ref[...], gain_ref[...]).astype(BF16)
    q_ref[...] = _dot(xn, w_ref[:, 0:hk])
    k_ref[...] = _dot(xn, w_ref[:, hk:2 * hk])
    v_ref[...] = _dot(xn, w_ref[:, 2 * hk:2 * hk + hv]).astype(BF16)
    r_ref[...] = _dot(xn, w_ref[:, 2 * hk + hv:2 * hk + 2 * hv])
    lr = _dot(xn, wlr_ref[...])
    z = _dot(lr.astype(BF16), wa_ref[...]) + ba_ref[...]
    g_ref[...] = (jnp.minimum(z, 0.0) - jnp.log(1.0 + jnp.exp(-jnp.abs(z)))) / GLA_TAU


def _gla_in(x2, gain, w_main, w_lr, w_alpha, b_alpha):
    T = x2.shape[0]
    hk = GLA_HEADS * GLA_DK
    hv = GLA_HEADS * GLA_DV
    tm = TOKEN_TILE
    row = lambda n: pl.BlockSpec((tm, n), lambda i: (i, 0))
    full = lambda a: pl.BlockSpec(a.shape, lambda i: (0,) * a.ndim)
    return pl.pallas_call(
        _gla_in_kernel,
        grid=(T // tm,),
        in_specs=[row(D_MODEL), full(gain), full(w_main), full(w_lr), full(w_alpha), full(b_alpha)],
        out_specs=[row(hk), row(hk), row(hv), row(hv), row(hk)],
        out_shape=[jax.ShapeDtypeStruct((T, hk), F32), jax.ShapeDtypeStruct((T, hk), F32),
                   jax.ShapeDtypeStruct((T, hv), BF16), jax.ShapeDtypeStruct((T, hv), F32),
                   jax.ShapeDtypeStruct((T, hk), F32)],
        compiler_params=_cparams(),
        name="gla_in",
    )(x2, gain, w_main, w_lr, w_alpha, b_alpha)


def _gla_scan_kernel(q_ref, k_ref, g_ref, v_ref, r_ref, gain_ref, o_ref, st_ref):
    C = GLA_CHUNK
    nb = q_ref.shape[0]

    @pl.when(pl.program_id(0) == 0)
    def _():
        st_ref[...] = jnp.zeros_like(st_ref)

    ri = lax.broadcasted_iota(jnp.int32, (C, C), 0)
    ci = lax.broadcasted_iota(jnp.int32, (C, C), 1)
    tri = ri >= ci
    tri_b = jnp.where(tri, 1.0, 0.0).astype(BF16)
    for b in range(nb):
        g1, g2, g3 = _split3(g_ref[b])
        bc = _dot(tri_b, g1) + _dot(tri_b, g2) + _dot(tri_b, g3)
        eb = jnp.exp(bc)
        enb = jnp.exp(-bc)
        ebl = jnp.exp(bc[C - 1:C, :])
        qd = q_ref[b] * (GLA_DK ** -0.5) * eb
        ki = k_ref[b] * enb
        kf = ki * ebl
        for h in range(GLA_HEADS):
            ks = slice(h * GLA_DK, (h + 1) * GLA_DK)
            vs = slice(h * GLA_DV, (h + 1) * GLA_DV)
            qd_h = qd[:, ks].astype(BF16)
            ki_h = ki[:, ks].astype(BF16)
            kf_h = kf[:, ks].astype(BF16)
            v_h = v_ref[b, :, vs]
            att = jnp.where(tri, _dot_nt(qd_h, ki_h), 0.0).astype(BF16)
            st = st_ref[b * GLA_HEADS + h]
            o = _dot(att, v_h) + _dot_nt(qd_h, st.astype(BF16))
            st_ref[b * GLA_HEADS + h] = st * ebl[:, ks] + _dot_tn(v_h, kf_h)
            ms = jnp.mean(o * o, axis=-1, keepdims=True)
            on = o * lax.rsqrt(ms + RMS_EPS) * gain_ref[:, vs]
            rr = r_ref[b, :, vs]
            o_ref[b, :, vs] = (on * (rr / (1.0 + jnp.exp(-rr)))).astype(BF16)


def _gla_scan(q, k, g, v, r, gain):
    B, S, hk = q.shape
    hv = v.shape[2]
    C = GLA_CHUNK
    blk = lambda n: pl.BlockSpec((B, C, n), lambda c: (0, c, 0))
    return pl.pallas_call(
        _gla_scan_kernel,
        grid=(S // C,),
        in_specs=[blk(hk), blk(hk), blk(hk), blk(hv), blk(hv),
                  pl.BlockSpec(gain.shape, lambda c: (0, 0))],
        out_specs=blk(hv),
        out_shape=jax.ShapeDtypeStruct((B, S, hv), BF16),
        scratch_shapes=[pltpu.VMEM((B * GLA_HEADS, GLA_DV, GLA_DK), F32)],
        compiler_params=_cparams(),
        name="gla_scan",
    )(q, k, g, v, r, gain)


def _proj_res_kernel(a_ref, w_ref, res_ref, o_ref):
    o_ref[...] = res_ref[...] + _dot(a_ref[...], w_ref[...])


def _proj_bias_res_kernel(a_ref, w_ref, b_ref, res_ref, o_ref):
    o_ref[...] = res_ref[...] + (_dot(a_ref[...], w_ref[...]) + b_ref[...])


def _proj_res(a, w, bias, res, name):
    T, K = a.shape
    N = w.shape[1]
    tm = TOKEN_TILE
    row = lambda n: pl.BlockSpec((tm, n), lambda i: (i, 0))
    wspec = pl.BlockSpec((K, N), lambda i: (0, 0))
    if bias is None:
        body, specs, args = _proj_res_kernel, [row(K), wspec, row(N)], (a, w, res)
    else:
        body = _proj_bias_res_kernel
        specs = [row(K), wspec, pl.BlockSpec((1, N), lambda i: (0, 0)), row(N)]
        args = (a, w, bias, res)
    return pl.pallas_call(
        body,
        grid=(T // tm,),
        in_specs=specs,
        out_specs=row(N),
        out_shape=jax.ShapeDtypeStruct((T, N), F32),
        compiler_params=_cparams(),
        name=name,
    )(*args)


def _swa_in_kernel(h_ref, pos_ref, gain_ref, w_ref, b_ref, pq_ref, pk_ref, qg_ref, kg_ref,
                   freq_ref, q_ref, k_ref, v_ref):
    nq = SWA_QH * SWA_HD
    nk = 2 * SWA_KVH * SWA_HD
    xn = _rms(h_ref[...], gain_ref[...]).astype(BF16)
    proj = _dot(xn, w_ref[...]) + b_ref[...]
    q = proj[:, 0:nq]
    k = proj[:, nq:nq + nk]
    v = proj[:, nq + nk:nq + 2 * nk]

    def headnorm(t, p_ref, gain):
        s1, s2 = _split2(t * t)
        ms = _dot(s1, p_ref[...]) + _dot(s2, p_ref[...])
        return t * lax.rsqrt(ms + RMS_EPS) * gain

    q = headnorm(q, pq_ref, qg_ref[...])
    k = headnorm(k, pk_ref, kg_ref[...])

    ang = pos_ref[...].astype(F32) * freq_ref[...]
    cs = jnp.cos(ang)
    sn = jnp.sin(ang)
    hl = lax.broadcasted_iota(jnp.int32, ang.shape, 1) & (SWA_HD - 1)
    half = ROPE_DIM // 2
    s_lo = jnp.where(hl < half, -sn, 0.0)
    s_hi = jnp.where((hl >= half) & (hl < ROPE_DIM), sn, 0.0)

    def rope(t):
        outs = []
        for c in range(t.shape[1] // LANES):
            x = t[:, c * LANES:(c + 1) * LANES]
            x_up = pltpu.roll(x, LANES - half, axis=1)
            x_dn = pltpu.roll(x, half, axis=1)
            outs.append(x * cs + x_up * s_lo + x_dn * s_hi)
        return jnp.concatenate(outs, axis=1)

    q_ref[...] = (rope(q) * (SWA_HD ** -0.5)).astype(BF16)
    k_ref[...] = rope(k).astype(BF16)
    v_ref[...] = v.astype(BF16)


def _swa_in(h2, pos, gain, w, b, pq, pk, qg, kg, freq):
    T = h2.shape[0]
    nq = SWA_QH * SWA_HD
    nk = 2 * SWA_KVH * SWA_HD
    tm = TOKEN_TILE
    row = lambda n: pl.BlockSpec((tm, n), lambda i: (i, 0))
    full = lambda a: pl.BlockSpec(a.shape, lambda i: (0,) * a.ndim)
    return pl.pallas_call(
        _swa_in_kernel,
        grid=(T // tm,),
        in_specs=[row(D_MODEL), row(1), full(gain), full(w), full(b), full(pq), full(pk),
                  full(qg), full(kg), full(freq)],
        out_specs=[row(nq), row(nk), row(nk)],
        out_shape=[jax.ShapeDtypeStruct((T, nq), BF16), jax.ShapeDtypeStruct((T, nk), BF16),
                   jax.ShapeDtypeStruct((T, nk), BF16)],
        compiler_params=_cparams(),
        name="swa_in",
    )(h2, pos, gain, w, b, pq, pk, qg, kg, freq)


def _swa_attn_kernel(sink_ref, q_ref, kc_ref, kp_ref, vc_ref, vp_ref, o_ref):
    n = pl.program_id(1)
    blk = SWA_BLK
    kk = jnp.concatenate([kp_ref[...], kc_ref[...]], axis=0)
    vv = jnp.concatenate([vp_ref[...], vc_ref[...]], axis=0)
    qi = lax.broadcasted_iota(jnp.int32, (blk, 2 * blk), 0)
    ki = lax.broadcasted_iota(jnp.int32, (blk, 2 * blk), 1)
    rel = qi + blk - ki
    first_key = jnp.where(n > 0, 0, blk)
    mask = (rel >= 0) & (rel < blk) & (ki >= first_key)
    lane = lax.broadcasted_iota(jnp.int32, (blk, LANES), 1)
    first = lane < SWA_HD
    for j in range(SWA_KVH):
        k2 = kk[:, j * LANES:(j + 1) * LANES]
        v2 = vv[:, j * LANES:(j + 1) * LANES]
        for gp in range(SWA_GROUP // 2):
            c0 = j * (SWA_GROUP * SWA_HD) + gp * LANES
            qp = q_ref[:, c0:c0 + LANES]
            outs = []
            for hf in range(2):
                qm = jnp.where(first if hf == 0 else ~first, qp, jnp.zeros_like(qp))
                s = jnp.where(mask, _dot_nt(qm, k2), -jnp.inf)
                sink = sink_ref[j * SWA_GROUP + 2 * gp + hf]
                m = jnp.maximum(jnp.max(s, axis=-1, keepdims=True), sink)
                p = jnp.exp(s - m)
                denom = jnp.sum(p, axis=-1, keepdims=True) + jnp.exp(sink - m)
                outs.append(_dot((p / denom).astype(BF16), v2))
            o_ref[:, c0:c0 + LANES] = jnp.where(first, outs[0], outs[1]).astype(BF16)


def _swa_attn(sinks, q, k2, v2):
    B, S, nq = q.shape
    nk = k2.shape[2]
    blk = SWA_BLK
    cur = lambda n_: pl.BlockSpec((None, blk, n_), lambda b, n: (b, n, 0))
    prev = lambda n_: pl.BlockSpec((None, blk, n_), lambda b, n: (b, jnp.maximum(n - 1, 0), 0))
    return pl.pallas_call(
        _swa_attn_kernel,
        grid=(B, S // blk),
        in_specs=[pl.BlockSpec(memory_space=pltpu.SMEM), cur(nq), cur(nk), prev(nk), cur(nk), prev(nk)],
        out_specs=cur(nq),
        out_shape=jax.ShapeDtypeStruct((B, S, nq), BF16),
        compiler_params=_cparams(2),
        name="swa_attn",
    )(sinks, q, k2, k2, v2, v2)


def _top16_rows(a, nrows):
    rid = lax.broadcasted_iota(jnp.int32, a.shape, 0)
    vals, ids = [], []
    for _ in range(PEER_TOPK):
        m = jnp.max(a, axis=0, keepdims=True)
        am = jnp.min(jnp.where(a == m, rid, nrows), axis=0, keepdims=True)
        vals.append(m)
        ids.append(am)
        a = jnp.where(rid == am, -jnp.inf, a)
    return jnp.concatenate(vals, axis=0), jnp.concatenate(ids, axis=0)


def _staircase_blocks():
    K = PEER_TOPK
    blocks = []
    for i in range(4):
        nj = K // (i + 1)
        for j0 in range(0, nj, 8):
            blocks.append(("j", i, j0, [j0 + r < nj for r in range(8)]))
    for j in range(3):
        for i0 in (0, 8):
            keep = [(i0 + r >= 4) and ((i0 + r + 1) * (j + 1) <= K) for r in range(8)]
            if any(keep):
                blocks.append(("i", j, i0, keep))
    return blocks


_STAIR = _staircase_blocks()


def _peer_select_kernel(h_ref, gain_ref, wq_ref, kbd_ref, xn_ref, idx_ref, gate_ref):
    K = PEER_TOPK
    tm = h_ref.shape[0]
    xn = _rms(h_ref[...], gain_ref[...])
    xn_ref[...] = xn
    qv = _dot(xn.astype(BF16), wq_ref[...]).astype(BF16)
    st = _dot_nt(kbd_ref[...], qv)
    sub = lax.broadcasted_iota(jnp.int32, (8, tm), 0)
    experts = []
    for h in range(PEER_HEADS):
        r0 = 2 * h * PEER_KEYS
        v1, i1 = _top16_rows(st[r0:r0 + PEER_KEYS], PEER_KEYS)
        v2, i2 = _top16_rows(st[r0 + PEER_KEYS:r0 + 2 * PEER_KEYS], PEER_KEYS)
        cv, cc = [], []
        for kind, fixed, start, keep in _STAIR:
            keepm = functools.reduce(jnp.logical_or, [sub == r for r in range(8) if keep[r]])
            if kind == "j":
                val = v1[fixed:fixed + 1] + v2[start:start + 8]
                code = fixed * K + start + sub
            else:
                val = v1[start:start + 8] + v2[fixed:fixed + 1]
                code = (start + sub) * K + fixed
            cv.append(jnp.where(keepm, val, -jnp.inf))
            cc.append(code)
        cand = jnp.concatenate(cv, axis=0)
        code = jnp.concatenate(cc, axis=0)
        scs, cis = [], []
        for _ in range(K):
            m = jnp.max(cand, axis=0, keepdims=True)
            cm = jnp.min(jnp.where(cand == m, code, K * K), axis=0, keepdims=True)
            scs.append(m)
            cis.append(cm)
            cand = jnp.where(code == cm, -jnp.inf, cand)
        sc = jnp.concatenate(scs, axis=0)
        ci = jnp.concatenate(cis, axis=0)
        chi = ci >> 4
        clo = ci & (K - 1)
        e1 = jnp.zeros_like(ci)
        e2 = jnp.zeros_like(ci)
        for i in range(K):
            e1 = jnp.where(chi == i, i1[i:i + 1], e1)
            e2 = jnp.where(clo == i, i2[i:i + 1], e2)
        experts.append((e1 * PEER_KEYS + e2) * TABLE_ROWS)
        ex = jnp.exp(sc - sc[0:1])
        gate_ref[h * K:(h + 1) * K, :] = ex / jnp.sum(ex, axis=0, keepdims=True)
    idx_ref[...] = jnp.concatenate(experts, axis=0).T


def _peer_select(h2, gain, wq, kbd):
    T = h2.shape[0]
    tm = TOKEN_TILE
    full = lambda a: pl.BlockSpec(a.shape, lambda i: (0,) * a.ndim)
    return pl.pallas_call(
        _peer_select_kernel,
        grid=(T // tm,),
        in_specs=[pl.BlockSpec((tm, D_MODEL), lambda i: (i, 0)), full(gain), full(wq), full(kbd)],
        out_specs=[pl.BlockSpec((tm, D_MODEL), lambda i: (i, 0)),
                   pl.BlockSpec((tm, PEER_SLOTS), lambda i: (i, 0)),
                   pl.BlockSpec((PEER_SLOTS, tm), lambda i: (0, i))],
        out_shape=[jax.ShapeDtypeStruct((T, D_MODEL), F32),
                   jax.ShapeDtypeStruct((T, PEER_SLOTS), jnp.int32),
                   jax.ShapeDtypeStruct((PEER_SLOTS, T), F32)],
        compiler_params=_cparams(),
        name="peer_select",
    )(h2, gain, wq, kbd)


def _pack_table(w):
    wb = w.astype(BF16)
    half = D_MODEL // 2
    lo = lax.bitcast_convert_type(wb[:, :half], jnp.uint16).astype(jnp.uint32)
    hi = lax.bitcast_convert_type(wb[:, half:], jnp.uint16).astype(jnp.uint32)
    return (lo | (hi << 16)).reshape(w.shape[0], 4, LANES)


def _unpack(w):
    lo = lax.bitcast_convert_type(w << 16, F32)
    hi = lax.bitcast_convert_type(w & jnp.uint32(0xFFFF0000), F32)
    return lo, hi


def _pack_pair(lo, hi):
    return pltpu.bitcast(pltpu.pack_elementwise([lo, hi], packed_dtype=BF16), BF16)


def _pair_products(words, xb):
    prod = pltpu.bitcast(pltpu.bitcast(words, BF16) * xb, jnp.uint32)
    plo = pltpu.unpack_elementwise(prod, index=0, packed_dtype=BF16, unpacked_dtype=F32)
    phi = pltpu.unpack_elementwise(prod, index=1, packed_dtype=BF16, unpacked_dtype=F32)
    return plo, phi


def _expert_rows(tab_ref, row):
    return tab_ref[pl.ds(pl.multiple_of(row, TABLE_ROWS), TABLE_ROWS), :]


def _stream_index_groups(idx_ref, ibufs, sems, process):
    ng = PEER_TOKENS // IDX_GROUP

    def copy(g, slot):
        rows = pl.ds(pl.multiple_of(g * IDX_GROUP, IDX_GROUP), IDX_GROUP)
        return pltpu.make_async_copy(idx_ref.at[rows], ibufs[slot], sems.at[slot])

    copy(0, 0).start()

    def pair(p, carry):
        g = 2 * p
        copy(g + 1, 1).start()
        copy(g, 0).wait()
        process(g, ibufs[0])
        nxt = jnp.minimum(g + 2, ng - 1)
        copy(nxt, 0).start()
        copy(g + 1, 1).wait()
        process(g + 1, ibufs[1])
        return carry

    lax.fori_loop(0, ng // 2, pair, 0)
    copy(ng - 1, 0).wait()


def _index_scratch():
    return [pltpu.SMEM((IDX_GROUP, PEER_SLOTS), jnp.int32), pltpu.SMEM((IDX_GROUP, PEER_SLOTS), jnp.int32),
            pltpu.SemaphoreType.DMA((2,))]


def _peer_u_kernel(idx_ref, xlo_ref, xhi_ref, gate_ref, tab_ref, a_ref,
                   s0_ref, s1_ref, ib0_ref, ib1_ref, sems):
    tb = PEER_TOKENS
    ss = FOLD_STRIDE
    lane = lax.broadcasted_iota(jnp.int32, (8, tb), 1)
    folds = (s0_ref, s1_ref)

    def gather(t, j, ibuf, s_ref):
        xb = _pack_pair(xlo_ref[t], xhi_ref[t])
        for k in range(PEER_SLOTS):
            plo, phi = _pair_products(_expert_rows(tab_ref, ibuf[j, k]), xb)
            s_ref[pl.ds(k, 4, stride=ss), :] = plo + phi

    def fold(t, s_ref):
        sel = lane == t
        for j in range(PEER_SLOTS // 8):
            r = s_ref[pl.ds(8 * j, 8), :]
            for s in range(1, 4):
                r = r + s_ref[pl.ds(s * ss + 8 * j, 8), :]
            tot = jnp.sum(r, axis=1, keepdims=True)
            a_ref[pl.ds(8 * j, 8), :] = jnp.where(sel, tot, a_ref[pl.ds(8 * j, 8), :])

    def process(g, ibuf):
        for j in range(IDX_GROUP):
            t = g * IDX_GROUP + j
            gather(t, j, ibuf, folds[j % 2])
            fold(t - 1, folds[(j + 1) % 2])

    a_ref[...] = jnp.zeros_like(a_ref)
    s1_ref[...] = jnp.zeros_like(s1_ref)
    _stream_index_groups(idx_ref, (ib0_ref, ib1_ref), sems, process)
    fold(tb - 1, folds[(tb - 1) % 2])
    z = a_ref[...]
    a_ref[...] = 0.5 * z * (1.0 + lax.erf(z * (2.0 ** -0.5))) * gate_ref[...]


def _peer_u(idx, xlo, xhi, gate_t, tab):
    T = idx.shape[0]
    tb = PEER_TOKENS
    ne = tab.shape[0]
    return pl.pallas_call(
        _peer_u_kernel,
        grid=(T // tb,),
        in_specs=[
            pl.BlockSpec((tb, PEER_SLOTS), lambda i: (i, 0)),
            pl.BlockSpec((tb, 4, LANES), lambda i: (i, 0, 0)),
            pl.BlockSpec((tb, 4, LANES), lambda i: (i, 0, 0)),
            pl.BlockSpec((PEER_SLOTS, tb), lambda i: (0, i)),
            pl.BlockSpec((ne * 4, LANES), lambda i: (0, 0), pipeline_mode=pl.Buffered(1)),
        ],
        out_specs=pl.BlockSpec((PEER_SLOTS, tb), lambda i: (0, i)),
        out_shape=jax.ShapeDtypeStruct((PEER_SLOTS, T), F32),
        scratch_shapes=[pltpu.VMEM((4 * FOLD_STRIDE, LANES), F32),
                        pltpu.VMEM((4 * FOLD_STRIDE, LANES), F32)] + _index_scratch(),
        compiler_params=_cparams(),
        name="peer_u",
    )(idx, xlo, xhi, gate_t, tab.reshape(ne * 4, LANES))


def _peer_v_kernel(idx_ref, a_ref, rlo_ref, rhi_ref, tab_ref, olo_ref, ohi_ref,
                   b0_ref, b1_ref, ib0_ref, ib1_ref, sems):
    tb = PEER_TOKENS
    lane = lax.broadcasted_iota(jnp.int32, (PEER_SLOTS, tb), 1)
    bcast = (b0_ref, b1_ref)

    def prep(t, b_ref):
        col = jnp.sum(jnp.where(lane == t, a_ref[...], 0.0), axis=1, keepdims=True)
        wide = jnp.broadcast_to(col, (PEER_SLOTS, LANES))
        b_ref[...] = pltpu.pack_elementwise([wide, wide], packed_dtype=BF16)

    def accum(t, j, ibuf, b_ref):
        zero = jnp.zeros((TABLE_ROWS, LANES), F32)
        acc_lo = [rlo_ref[t]] + [zero] * (V_CHAINS - 1)
        acc_hi = [rhi_ref[t]] + [zero] * (V_CHAINS - 1)
        for k in range(PEER_SLOTS):
            ab = pltpu.bitcast(jnp.broadcast_to(b_ref[pl.ds(k, 1), :], (TABLE_ROWS, LANES)), BF16)
            plo, phi = _pair_products(_expert_rows(tab_ref, ibuf[j, k]), ab)
            acc_lo[k % V_CHAINS] = acc_lo[k % V_CHAINS] + plo
            acc_hi[k % V_CHAINS] = acc_hi[k % V_CHAINS] + phi
        olo_ref[t] = functools.reduce(jnp.add, acc_lo)
        ohi_ref[t] = functools.reduce(jnp.add, acc_hi)

    def process(g, ibuf):
        for j in range(IDX_GROUP):
            t = g * IDX_GROUP + j
            prep(jnp.minimum(t + 1, tb - 1), bcast[(j + 1) % 2])
            accum(t, j, ibuf, bcast[j % 2])

    prep(0, b0_ref)
    _stream_index_groups(idx_ref, (ib0_ref, ib1_ref), sems, process)


def _peer_v(idx, a_t, rlo, rhi, tab):
    T = idx.shape[0]
    tb = PEER_TOKENS
    ne = tab.shape[0]
    tok = pl.BlockSpec((tb, 4, LANES), lambda i: (i, 0, 0))
    return pl.pallas_call(
        _peer_v_kernel,
        grid=(T // tb,),
        in_specs=[
            pl.BlockSpec((tb, PEER_SLOTS), lambda i: (i, 0)),
            pl.BlockSpec((PEER_SLOTS, tb), lambda i: (0, i)),
            tok, tok,
            pl.BlockSpec((ne * 4, LANES), lambda i: (0, 0), pipeline_mode=pl.Buffered(1)),
        ],
        out_specs=[tok, tok],
        out_shape=[jax.ShapeDtypeStruct((T, 4, LANES), F32), jax.ShapeDtypeStruct((T, 4, LANES), F32)],
        scratch_shapes=[pltpu.VMEM((PEER_SLOTS, LANES), jnp.uint32), pltpu.VMEM((PEER_SLOTS, LANES), jnp.uint32)]
        + _index_scratch(),
        compiler_params=_cparams(),
        name="peer_v",
    )(idx, a_t, rlo, rhi, tab.reshape(ne * 4, LANES))


def _halves(a2):
    a4 = a2.reshape(a2.shape[0], 2, 4, LANES)
    return a4[:, 0], a4[:, 1]


def _peer_layer(h2, gain, wq, keys, u, v):
    T = h2.shape[0]
    eye = jnp.eye(PEER_HEADS * 2, dtype=F32)
    kbd = jnp.einsum("pkd,pq->pkqd", keys.reshape(PEER_HEADS * 2, PEER_KEYS, PEER_HALF), eye)
    kbd = kbd.reshape(PEER_HEADS * 2 * PEER_KEYS, D_MODEL).astype(BF16)
    xn, idx, gate_t = _peer_select(h2, gain.reshape(1, D_MODEL), wq.astype(BF16), kbd)
    xlo, xhi = _halves(xn)
    a_t = _peer_u(idx, xlo, xhi, gate_t, _pack_table(u))
    rlo, rhi = _halves(h2)
    olo, ohi = _peer_v(idx, a_t, rlo, rhi, _pack_table(v))
    return jnp.concatenate([olo.reshape(T, D_MODEL // 2), ohi.reshape(T, D_MODEL // 2)], axis=1)


def _gla_layer(x2, B, S, gain, w_in, w_alpha, b_alpha, norm_gain, w_out):
    hk = GLA_HEADS * GLA_DK
    hv = GLA_HEADS * GLA_DV
    w_main = w_in[:, :2 * hk + 2 * hv].astype(BF16)
    w_lr = jnp.pad(w_in[:, 2 * hk + 2 * hv:], ((0, 0), (0, LANES - GLA_RANK))).astype(BF16)
    wa = jnp.pad(w_alpha, ((0, LANES - GLA_RANK), (0, 0))).astype(BF16)
    q, k, v, r, g = _gla_in(x2, gain.reshape(1, D_MODEL), w_main, w_lr, wa, b_alpha.reshape(1, hk))
    sh = lambda a: a.reshape(B, S, a.shape[1])
    og = _gla_scan(sh(q), sh(k), sh(g), sh(v), sh(r), norm_gain.reshape(1, hv))
    return _proj_res(og.reshape(B * S, hv), w_out.astype(BF16), None, x2, "gla_out")


def _swa_layer(h2, pos, B, S, gain, w_in, b_in, q_gain, k_gain, sinks, w_out, b_out):
    nq = SWA_QH * SWA_HD
    hd = SWA_HD
    wq_, wk_, wv_ = w_in[:, :nq], w_in[:, nq:nq + SWA_KVH * hd], w_in[:, nq + SWA_KVH * hd:]
    bq_, bk_, bv_ = b_in[:nq], b_in[nq:nq + SWA_KVH * hd], b_in[nq + SWA_KVH * hd:]

    def dup(a):
        parts = []
        for j in range(SWA_KVH):
            parts += [a[..., j * hd:(j + 1) * hd]] * 2
        return jnp.concatenate(parts, axis=-1)

    w = jnp.concatenate([wq_, dup(wk_), dup(wv_)], axis=1).astype(BF16)
    b = jnp.concatenate([bq_, dup(bk_), dup(bv_)]).reshape(1, -1)
    nk = 2 * SWA_KVH * hd
    pmat = lambda n: (jnp.kron(jnp.eye(n // hd, dtype=F32), jnp.ones((hd, hd), F32)) / hd).astype(BF16)
    inv_freq = ROPE_THETA ** (-jnp.arange(0, ROPE_DIM, 2, dtype=F32) / ROPE_DIM)
    hl = np.arange(LANES) % hd
    freq = jnp.where(hl < ROPE_DIM, inv_freq[hl % (ROPE_DIM // 2)], 0.0).reshape(1, LANES)
    q, k2, v2 = _swa_in(h2, pos.reshape(B * S, 1), gain.reshape(1, D_MODEL), w, b, pmat(nq), pmat(nk),
                        jnp.tile(q_gain, nq // hd).reshape(1, nq), jnp.tile(k_gain, nk // hd).reshape(1, nk), freq)
    o = _swa_attn(sinks, q.reshape(B, S, nq), k2.reshape(B, S, nk), v2.reshape(B, S, nk))
    return _proj_res(o.reshape(B * S, nq), w_out.astype(BF16), b_out.reshape(1, D_MODEL), h2, "swa_out")


def kernel(x, positions, ln_mix, ln_ffn, gla_w_in, gla_w_alpha, gla_b_alpha, gla_norm, gla_w_out, swa_w_in, swa_b_in, swa_q_norm, swa_k_norm, swa_sinks, swa_w_out, swa_b_out, peer_w_q, peer_keys, peer_u, peer_v):
    B, S, D = x.shape
    h = x.reshape(B * S, D)
    h = _gla_layer(h, B, S, ln_mix[0], gla_w_in[0], gla_w_alpha[0], gla_b_alpha[0], gla_norm[0], gla_w_out[0])
    h = _peer_layer(h, ln_ffn[0], peer_w_q[0], peer_keys[0], peer_u[0], peer_v[0])
    h = _swa_layer(h, positions, B, S, ln_mix[1], swa_w_in[0], swa_b_in[0], swa_q_norm[0], swa_k_norm[0],
                   swa_sinks[0], swa_w_out[0], swa_b_out[0])
    h = _peer_layer(h, ln_ffn[1], peer_w_q[1], peer_keys[1], peer_u[1], peer_v[1])
    return h.reshape(B, S, D)
```
